```python
import math
import jax
import jax.numpy as jnp
from jax import lax
import numpy as np

D_MODEL = 2048
BATCH = 2
SEQ = 16384
DEPTH = 1

MIX_WIDTH = D_MODEL
HEAD_DIM = 128
GDN_WIDTH = MIX_WIDTH // 2
GDN_HEADS = GDN_WIDTH // HEAD_DIM
GDN_CONV = 4
CHUNK = 64
SC_WIDTH = MIX_WIDTH - GDN_WIDTH
SC_CONV = 3
IN_COLS = 4 * GDN_WIDTH + 2 * GDN_HEADS + 3 * SC_WIDTH
IN_SPLITS = (3 * GDN_WIDTH, 4 * GDN_WIDTH, 4 * GDN_WIDTH + GDN_HEADS, 4 * GDN_WIDTH + 2 * GDN_HEADS)
N_GROUPS = 8
EXPERTS_PER_GROUP = 8
N_EXPERTS = N_GROUPS * EXPERTS_PER_GROUP
TOP_K = 2
D_EXPERT = D_MODEL // 4
MOE_BLOCK = 128
EPS = 1e-6

kernel_name = "hybrid_gdn_shortconv_hier_moe"


def rms_norm(x, w):
    xf = x.astype(jnp.float32)
    y = xf * lax.rsqrt(jnp.mean(xf * xf, axis=-1, keepdims=True) + EPS)
    return (y * w.astype(jnp.float32)).astype(x.dtype)


def l2_normalize(x):
    return x * lax.rsqrt(jnp.sum(x * x, axis=-1, keepdims=True) + EPS)


def causal_depthwise_conv(x, w):
    width = w.shape[0]
    seq = x.shape[1]
    xp = jnp.pad(x, ((0, 0), (width - 1, 0), (0, 0)))
    y = xp[:, 0:seq, :] * w[0]
    for j in range(1, width):
        y = y + xp[:, j:j + seq, :] * w[j]
    return y


def chunk_gated_delta_rule(q, k, v, g, beta):
    bsz, seq, heads, dk = q.shape
    dv = v.shape[-1]
    n_chunks = seq // CHUNK

    def chunks(t):
        t = t.reshape((bsz, n_chunks, CHUNK, heads) + t.shape[3:])
        return jnp.moveaxis(t, (1, 3), (0, 2))

    qc, kc, vc = chunks(q), chunks(k), chunks(v)
    bc = chunks(beta)
    gc = jnp.cumsum(chunks(g), axis=-1)
    idx = jnp.arange(CHUNK)
    causal = idx[:, None] >= idx[None, :]
    strict = idx[:, None] > idx[None, :]
    decay = jnp.exp(jnp.where(causal, gc[..., :, None] - gc[..., None, :], -jnp.inf))

    kb = kc * bc[..., None]
    a_mat = jnp.where(strict, jnp.einsum('nbhid,nbhjd->nbhij', kb, kc) * decay, 0.0)
    rhs = jnp.concatenate([vc * bc[..., None], kb * jnp.exp(gc)[..., None]], axis=-1)
    sol = lax.linalg.triangular_solve(a_mat, rhs, left_side=True, lower=True, unit_diagonal=True)
    u, w = sol[..., :dv], sol[..., dv:]
    attn = jnp.einsum('nbhid,nbhjd->nbhij', qc, kc) * decay
    q_dec = qc * jnp.exp(gc)[..., None]
    k_dec = kc * jnp.exp(gc[..., -1:] - gc)[..., None]
    chunk_decay = jnp.exp(gc[..., -1])

    def step(state, xs):
        u_i, w_i, q_i, k_i, a_i, d_i = xs
        v_new = u_i - jnp.einsum('bhck,bhkv->bhcv', w_i, state)
        o_i = jnp.einsum('bhck,bhkv->bhcv', q_i, state) + jnp.einsum('bhij,bhjv->bhiv', a_i, v_new)
        state = state * d_i[..., None, None] + jnp.einsum('bhck,bhcv->bhkv', k_i, v_new)
        return state, o_i

    s0 = jnp.zeros((bsz, heads, dk, dv), jnp.float32)
    _, o = lax.scan(step, s0, (u, w, q_dec, k_dec, attn, chunk_decay))
    return jnp.moveaxis(o, (0, 2), (1, 3)).reshape(bsz, seq, heads, dv)


def gated_deltanet(qkv, z, b, a, conv_w, a_log, dt_bias, norm_w):
    bsz, seq, _ = qkv.shape

    def heads(t):
        return t.reshape(bsz, seq, GDN_HEADS, HEAD_DIM)

    qkv = jax.nn.silu(causal_depthwise_conv(qkv, conv_w)).astype(jnp.float32)
    q, k, v = jnp.split(qkv, 3, axis=-1)
    q = l2_normalize(heads(q)) * (HEAD_DIM ** -0.5)
    k = l2_normalize(heads(k))
    v = heads(v)
    beta = jax.nn.sigmoid(b.astype(jnp.float32))
    g = -jnp.exp(a_log.astype(jnp.float32)) * jax.nn.softplus(a.astype(jnp.float32) + dt_bias.astype(jnp.float32))
    o = chunk_gated_delta_rule(q, k, v, g, beta)
    o = o * lax.rsqrt(jnp.mean(o * o, axis=-1, keepdims=True) + EPS) * norm_w.astype(jnp.float32)
    o = o * jax.nn.silu(heads(z).astype(jnp.float32))
    return o.reshape(bsz, seq, GDN_WIDTH).astype(z.dtype)


def short_gated_conv(bch, conv_w):
    gate_b, gate_c, hid = jnp.split(bch, 3, axis=-1)
    return gate_b * causal_depthwise_conv(gate_c * hid, conv_w)


def token_mixer(hn, w_in, conv_qkv_w, a_log, dt_bias, gdn_norm_w, conv_sc_w, w_out):
    proj = hn @ w_in
    qkv, z, b, a, sc = jnp.split(proj, IN_SPLITS, axis=-1)
    y_gdn = gated_deltanet(qkv, z, b, a, conv_qkv_w, a_log, dt_bias, gdn_norm_w)
    y_sc = short_gated_conv(sc, conv_sc_w)
    return jnp.concatenate([y_gdn, y_sc], axis=-1) @ w_out


def hierarchical_moe(hn, wg_r, bg_r, we_r, be_r, w_gate, w_up, w_down):
    bsz, seq, d = hn.shape
    x2d = hn.reshape(bsz * seq, d)
    n_tok = x2d.shape[0]
    xf = x2d.astype(jnp.float32)
    group_prob = jax.nn.softmax(xf @ wg_r.astype(jnp.float32) + bg_r.astype(jnp.float32), axis=-1)
    g_idx = jnp.argmax(group_prob, axis=-1).astype(jnp.int32)
    g_w = jnp.take_along_axis(group_prob, g_idx[:, None], axis=-1)
    e_logits = (xf @ we_r.astype(jnp.float32) + be_r.astype(jnp.float32)).reshape(n_tok, N_GROUPS, EXPERTS_PER_GROUP)
    in_group = jnp.take_along_axis(e_logits, g_idx[:, None, None], axis=1)[:, 0]
    top_p, top_i = lax.top_k(jax.nn.softmax(in_group, axis=-1), TOP_K)
    gates = g_w * top_p / jnp.sum(top_p, axis=-1, keepdims=True)
    expert_ids = g_idx[:, None] * EXPERTS_PER_GROUP + top_i.astype(jnp.int32)

    n_assign = n_tok * TOP_K
    flat_e = expert_ids.reshape(n_assign)
    flat_tok = jnp.repeat(jnp.arange(n_tok, dtype=jnp.int32), TOP_K)
    flat_gate = gates.reshape(n_assign)
    order = jnp.argsort(flat_e)
    sorted_e = flat_e[order]
    counts = jnp.bincount(flat_e, length=N_EXPERTS)
    padded = (counts + MOE_BLOCK - 1) // MOE_BLOCK * MOE_BLOCK
    pad_end = jnp.cumsum(padded)
    pad_start = pad_end - padded
    start = jnp.cumsum(counts) - counts
    dest = pad_start[sorted_e] + (jnp.arange(n_assign, dtype=jnp.int32) - start[sorted_e])
    n_blocks = -(-n_assign // MOE_BLOCK) + N_EXPERTS
    n_rows = n_blocks * MOE_BLOCK
    tok_buf = jnp.zeros((n_rows,), jnp.int32).at[dest].set(flat_tok[order])
    gate_buf = jnp.zeros((n_rows,), jnp.float32).at[dest].set(flat_gate[order])
    block_expert = jnp.minimum(
        jnp.searchsorted(pad_end, jnp.arange(n_blocks, dtype=jnp.int32) * MOE_BLOCK, side='right'),
        N_EXPERTS - 1).astype(jnp.int32)

    def run_block(args):
        tok, e, gate = args
        xb = x2d[tok]
        hid = jax.nn.silu(xb @ w_gate[e]) * (xb @ w_up[e])
        return (hid @ w_down[e]).astype(jnp.float32) * gate[:, None]

    out = lax.map(run_block, (tok_buf.reshape(n_blocks, MOE_BLOCK), block_expert,
                              gate_buf.reshape(n_blocks, MOE_BLOCK)))
    y = jax.ops.segment_sum(out.reshape(n_rows, d), tok_buf, num_segments=n_tok)
    return y.astype(hn.dtype).reshape(bsz, seq, d)


def setup_inputs(seed: int = 0) -> dict:
    key = jax.random.key(seed)
    ks = jax.random.split(key, 20)
    f32 = jnp.float32
    L = DEPTH

    def normal(k, shape, scale):
        return jax.random.normal(k, shape, f32) * scale

    def gain(k, shape):
        return 1.0 + 0.05 * jax.random.normal(k, shape, f32)

    dt = jnp.exp(jax.random.uniform(ks[5], (L, GDN_HEADS), f32, math.log(1e-3), math.log(1e-1)))
    return {
        "x": jax.random.normal(ks[0], (BATCH, SEQ, D_MODEL), f32),
        "norm1_w": gain(ks[1], (L, D_MODEL)),
        "w_in": normal(ks[2], (L, D_MODEL, IN_COLS), D_MODEL ** -0.5),
        "conv_qkv_w": normal(ks[3], (L, GDN_CONV, 3 * GDN_WIDTH), GDN_CONV ** -0.5),
        "a_log": jnp.log(jax.random.uniform(ks[4], (L, GDN_HEADS), f32, 1.0, 16.0)),
        "dt_bias": dt + jnp.log(-jnp.expm1(-dt)),
        "gdn_norm_w": gain(ks[6], (L, HEAD_DIM)),
        "conv_sc_w": normal(ks[7], (L, SC_CONV, SC_WIDTH), SC_CONV ** -0.5),
        "w_out": normal(ks[8], (L, MIX_WIDTH, D_MODEL), MIX_WIDTH ** -0.5),
        "norm2_w": gain(ks[9], (L, D_MODEL)),
        "router_group_w": normal(ks[10], (L, D_MODEL, N_GROUPS), D_MODEL ** -0.5),
        "router_group_b": normal(ks[11], (L, N_GROUPS), 0.01),
        "router_expert_w": normal(ks[12], (L, D_MODEL, N_EXPERTS), D_MODEL ** -0.5),
        "router_expert_b": normal(ks[13], (L, N_EXPERTS), 0.01),
        "w_gate": normal(ks[14], (L, N_EXPERTS, D_MODEL, D_EXPERT), D_MODEL ** -0.5),
        "w_up": normal(ks[15], (L, N_EXPERTS, D_MODEL, D_EXPERT), D_MODEL ** -0.5),
        "w_down": normal(ks[16], (L, N_EXPERTS, D_EXPERT, D_MODEL), D_EXPERT ** -0.5),
        "final_norm_w": gain(ks[17], (D_MODEL,)),
    }


def reference(x, norm1_w, w_in, conv_qkv_w, a_log, dt_bias, gdn_norm_w, conv_sc_w, w_out,
              norm2_w, router_group_w, router_group_b, router_expert_w, router_expert_b,
              w_gate, w_up, w_down, final_norm_w):
    h = x
    for i in range(DEPTH):
        h = h + token_mixer(rms_norm(h, norm1_w[i]), w_in[i], conv_qkv_w[i], a_log[i], dt_bias[i],
                            gdn_norm_w[i], conv_sc_w[i], w_out[i])
        h = h + hierarchical_moe(rms_norm(h, norm2_w[i]), router_group_w[i], router_group_b[i],
                                 router_expert_w[i], router_expert_b[i], w_gate[i], w_up[i], w_down[i])
    return rms_norm(h, final_norm_w)
```

```python
import functools

import jax
import jax.numpy as jnp
from jax import lax
from jax.experimental import pallas as pl
from jax.experimental.pallas import tpu as pltpu

F32 = jnp.float32
BF16 = jnp.bfloat16
I32 = jnp.int32
U32 = jnp.uint32
HIGHEST = lax.Precision.HIGHEST

EPS = 1e-6
D_MODEL = 2048
HEAD_DIM = 128
GDN_HEADS = 8
GDN_WIDTH = GDN_HEADS * HEAD_DIM
GDN_CONV = 4
CHUNK = 64
SC_WIDTH = 1024
SC_CONV = 3
N_GROUPS = 8
EXPERTS_PER_GROUP = 8
N_EXPERTS = N_GROUPS * EXPERTS_PER_GROUP
TOP_K = 2
D_EXPERT = 512
LANES = 128
SUBLANES = 8
SC_TILE = 256
VMEM_LIMIT = 56 * 1024 * 1024


def _sigmoid(x):
    return 1.0 / (1.0 + jnp.exp(-x))


def _silu(x):
    return x * _sigmoid(x)


def _softplus(x):
    return jnp.maximum(x, 0.0) + jnp.log1p(jnp.exp(-jnp.abs(x)))


def _rms_scale(x):
    return x * lax.rsqrt(jnp.mean(x * x, axis=-1, keepdims=True) + EPS)


def _dot(a, b):
    return jnp.dot(a, b, preferred_element_type=F32)


def _hdot(a, b):
    return jnp.dot(a, b, preferred_element_type=F32, precision=HIGHEST)


def _params(*sem):
    return pltpu.CompilerParams(dimension_semantics=sem, vmem_limit_bytes=VMEM_LIMIT)


def _resident(shape):
    return pl.BlockSpec(shape, lambda *_: (0,) * len(shape), pipeline_mode=pl.Buffered(1))


def _qkvz_kernel(x_ref, n1w_ref, w_ref, wba_ref, cw_ref, alog_ref, dtb_ref, ltri_ref,
                 out_ref, bg_ref, buf_sc, carry_sc, *, tm):
    @pl.when(pl.program_id(1) == 0)
    def _():
        carry_sc[...] = jnp.zeros_like(carry_sc)

    hn = (_rms_scale(x_ref[...]) * n1w_ref[...]).astype(BF16)

    ba = _dot(hn, wba_ref[...])
    lane = lax.broadcasted_iota(I32, ba.shape, 1)
    g = -jnp.exp(alog_ref[...]) * _softplus(ba + dtb_ref[...])
    g = jnp.where((lane >= GDN_HEADS) & (lane < 2 * GDN_HEADS), g, 0.0)
    gc = _hdot(ltri_ref[...], g)
    bg_ref[...] = jnp.where(lane < GDN_HEADS, _sigmoid(ba), gc)

    for j in range(4):
        cols = slice(j * GDN_WIDTH, (j + 1) * GDN_WIDTH)
        p = _dot(hn, w_ref[:, cols])
        if j == 3:
            out_ref[:, cols] = p.astype(BF16)
            continue
        buf_sc[0:SUBLANES, :] = carry_sc[j]
        buf_sc[SUBLANES:SUBLANES + tm, :] = p
        carry_sc[j] = p[tm - SUBLANES:tm, :]
        cw = cw_ref[:, cols]
        y = cw[GDN_CONV - 1:GDN_CONV, :] * p
        for s in range(1, GDN_CONV):
            y = y + cw[GDN_CONV - 1 - s:GDN_CONV - s, :] * buf_sc[pl.ds(SUBLANES - s, tm), :]
        y = _silu(y)
        if j == 2:
            out_ref[:, cols] = y.astype(BF16)
            continue
        scale = HEAD_DIM ** -0.5 if j == 0 else 1.0
        for h in range(GDN_HEADS):
            hc = slice(h * HEAD_DIM, (h + 1) * HEAD_DIM)
            yh = y[:, hc]
            yh = yh * (lax.rsqrt(jnp.sum(yh * yh, axis=-1, keepdims=True) + EPS) * scale)
            out_ref[:, j * GDN_WIDTH + h * HEAD_DIM:j * GDN_WIDTH + (h + 1) * HEAD_DIM] = (
                yh.astype(BF16))


def _qkvz_call(x, n1w, w, wba, cw, alog, dtb, tm):
    bsz, seq, d = x.shape
    n = w.shape[1]
    idx = jnp.arange(tm)
    ltri = ((idx[:, None] >= idx[None, :]) &
            (idx[:, None] // CHUNK == idx[None, :] // CHUNK)).astype(F32)
    return pl.pallas_call(
        functools.partial(_qkvz_kernel, tm=tm),
        grid=(bsz, seq // tm),
        in_specs=[
            pl.BlockSpec((None, tm, d), lambda b, i: (b, i, 0)),
            _resident((1, d)),
            _resident((d, n)),
            _resident((d, LANES)),
            _resident((GDN_CONV, 3 * GDN_WIDTH)),
            _resident((1, LANES)),
            _resident((1, LANES)),
            _resident((tm, tm)),
        ],
        out_specs=[
            pl.BlockSpec((None, tm, n), lambda b, i: (b, i, 0)),
            pl.BlockSpec((None, tm, LANES), lambda b, i: (b, i, 0)),
        ],
        out_shape=[
            jax.ShapeDtypeStruct((bsz, seq, n), BF16),
            jax.ShapeDtypeStruct((bsz, seq, LANES), F32),
        ],
        scratch_shapes=[
            pltpu.VMEM((tm + SUBLANES, GDN_WIDTH), F32),
            pltpu.VMEM((3, SUBLANES, GDN_WIDTH), F32),
        ],
        compiler_params=_params("arbitrary", "arbitrary"),
        name="qkvz_proj",
    )(x, n1w, w, wba, cw, alog, dtb, ltri)


def _sc_kernel(x_ref, n1w_ref, w_ref, cw_ref, out_ref, buf_sc, carry_sc, *, tm):
    @pl.when(pl.program_id(1) == 0)
    def _():
        carry_sc[...] = jnp.zeros_like(carry_sc)

    hn = (_rms_scale(x_ref[...]) * n1w_ref[...]).astype(BF16)
    for t in range(SC_WIDTH // SC_TILE):
        p = _dot(hn, w_ref[:, 3 * SC_TILE * t:3 * SC_TILE * (t + 1)])
        gate_b = p[:, 0:SC_TILE]
        ch = p[:, SC_TILE:2 * SC_TILE] * p[:, 2 * SC_TILE:3 * SC_TILE]
        buf_sc[0:SUBLANES, :] = carry_sc[t]
        buf_sc[SUBLANES:SUBLANES + tm, :] = ch
        carry_sc[t] = ch[tm - SUBLANES:tm, :]
        cw = cw_ref[:, SC_TILE * t:SC_TILE * (t + 1)]
        y = cw[SC_CONV - 1:SC_CONV, :] * ch
        for s in range(1, SC_CONV):
            y = y + cw[SC_CONV - 1 - s:SC_CONV - s, :] * buf_sc[pl.ds(SUBLANES - s, tm), :]
        out_ref[:, SC_TILE * t:SC_TILE * (t + 1)] = (gate_b * y).astype(BF16)


def _sc_call(x, n1w, w, cw, tm):
    bsz, seq, d = x.shape
    return pl.pallas_call(
        functools.partial(_sc_kernel, tm=tm),
        grid=(bsz, seq // tm),
        in_specs=[
            pl.BlockSpec((None, tm, d), lambda b, i: (b, i, 0)),
            _resident((1, d)),
            _resident((d, 3 * SC_WIDTH)),
            _resident((SC_CONV, SC_WIDTH)),
        ],
        out_specs=pl.BlockSpec((None, tm, SC_WIDTH), lambda b, i: (b, i, 0)),
        out_shape=jax.ShapeDtypeStruct((bsz, seq, SC_WIDTH), BF16),
        scratch_shapes=[
            pltpu.VMEM((tm + SUBLANES, SC_TILE), F32),
            pltpu.VMEM((SC_WIDTH // SC_TILE, SUBLANES, SC_TILE), F32),
        ],
        compiler_params=_params("arbitrary", "arbitrary"),
        name="shortconv_proj",
    )(x, n1w, w, cw)


def _unit_lower_inverse(a, eye, m16, m32, m64):
    n = -(a * m16)
    p = eye + n
    for _ in range(3):
        n = _hdot(n, n)
        p = p + _hdot(p, n)
    p = p - _hdot(p, _hdot(a * m32, p))
    return p - _hdot(p, _hdot(a * m64, p))


def _gdn_kernel(q_ref, k_ref, v_ref, z_ref, bg_ref, gt_ref, nw_ref, out_ref, state_sc, *, tc):
    @pl.when(pl.program_id(1) == 0)
    def _():
        state_sc[...] = jnp.zeros_like(state_sc)

    row = lax.broadcasted_iota(I32, (CHUNK, CHUNK), 0)
    col = lax.broadcasted_iota(I32, (CHUNK, CHUNK), 1)
    causal = row >= col
    strict = row > col
    eye = (row == col).astype(F32)
    same16 = (row // 16) == (col // 16)
    same32 = (row // 32) == (col // 32)
    m16 = same16.astype(F32)
    m32 = (same32 & jnp.logical_not(same16)).astype(F32)
    m64 = jnp.logical_not(same32).astype(F32)
    nw = nw_ref[...]
    tt = (((0,), (0,)), ((), ()))
    nt = (((1,), (1,)), ((), ()))

    def chunk_body(c, carry):
        r0 = pl.multiple_of(c * CHUNK, CHUNK)
        rows = pl.ds(r0, CHUNK)
        bgc = bg_ref[rows, :]
        gt = gt_ref[c]
        for h in range(GDN_HEADS):
            hc = slice(h * HEAD_DIM, (h + 1) * HEAD_DIM)
            q = q_ref[rows, hc]
            k = k_ref[rows, hc]
            v = v_ref[rows, hc]
            beta = bgc[:, h:h + 1]
            gc = bgc[:, GDN_HEADS + h:GDN_HEADS + h + 1]
            gcr = gt[GDN_HEADS + h:GDN_HEADS + h + 1, :]
            glast = gc[CHUNK - 1:CHUNK, :]

            kq = lax.dot_general(jnp.concatenate([k, q], axis=0), k, nt,
                                 preferred_element_type=F32)
            decay = jnp.where(causal, jnp.exp(gc - gcr), 0.0)
            a = jnp.where(strict, kq[:CHUNK] * beta * decay, 0.0)
            attn = kq[CHUNK:] * decay
            t = _unit_lower_inverse(a, eye, m16, m32, m64)

            kf = k.astype(F32)
            eg = jnp.exp(gc)
            rhs = jnp.concatenate([v.astype(F32) * beta, kf * (beta * eg)], axis=1)
            uw = _hdot(t, rhs)
            u = uw[:, :HEAD_DIM]
            w = uw[:, HEAD_DIM:]
            q_dec = q.astype(F32) * eg
            k_dec = kf * jnp.exp(glast - gc)

            state = state_sc[h]
            wq = jnp.concatenate([w, q_dec], axis=0).astype(BF16)
            rs = _dot(wq, state.astype(BF16))
            v_new = u - rs[:CHUNK]
            vb = v_new.astype(BF16)
            o = rs[CHUNK:] + _dot(attn.astype(BF16), vb)
            state_sc[h] = state * jnp.exp(glast) + lax.dot_general(
                k_dec.astype(BF16), vb, tt, preferred_element_type=F32)

            o = _rms_scale(o) * nw * _silu(z_ref[rows, hc].astype(F32))
            out_ref[rows, hc] = o.astype(BF16)
        return carry

    lax.fori_loop(0, tc // CHUNK, chunk_body, 0)


def _gdn_call(qkvz, bg, gt, nw, tc):
    bsz, seq, _ = qkvz.shape

    def col(j):
        return pl.BlockSpec((None, tc, GDN_WIDTH), lambda b, i: (b, i, j))

    return pl.pallas_call(
        functools.partial(_gdn_kernel, tc=tc),
        grid=(bsz, seq // tc),
        in_specs=[
            col(0), col(1), col(2), col(3),
            pl.BlockSpec((None, tc, LANES), lambda b, i: (b, i, 0)),
            pl.BlockSpec((None, tc // CHUNK, 2 * GDN_HEADS, CHUNK), lambda b, i: (b, i, 0, 0)),
            _resident((1, HEAD_DIM)),
        ],
        out_specs=pl.BlockSpec((None, tc, GDN_WIDTH), lambda b, i: (b, i, 0)),
        out_shape=jax.ShapeDtypeStruct((bsz, seq, GDN_WIDTH), BF16),
        scratch_shapes=[pltpu.VMEM((GDN_HEADS, HEAD_DIM, HEAD_DIM), F32)],
        compiler_params=_params("arbitrary", "arbitrary"),
        name="gated_delta_rule",
    )(qkvz, qkvz, qkvz, qkvz, bg, gt, nw)


def _pack_bf16_pair(hi, lo):
    hb = lax.bitcast_convert_type(hi.astype(BF16).astype(F32), U32)
    lb = lax.bitcast_convert_type(lo.astype(BF16).astype(F32), U32)
    return (hb & jnp.uint32(0xFFFF0000)) | lax.shift_right_logical(lb, jnp.uint32(16))


def _unpack_bf16_pair(p):
    hi = lax.bitcast_convert_type(p & jnp.uint32(0xFFFF0000), F32).astype(BF16)
    lo = lax.bitcast_convert_type(lax.shift_left(p, jnp.uint32(16)), F32).astype(BF16)
    return hi, lo


def _outproj_kernel(yg_ref, ys_ref, x_ref, wo_ref, n2w_ref, wr_ref, rb_ref,
                    h1_ref, hp_ref, lt_ref):
    h1 = (x_ref[...] + _dot(yg_ref[...], wo_ref[0:GDN_WIDTH, :])
          + _dot(ys_ref[...], wo_ref[GDN_WIDTH:GDN_WIDTH + SC_WIDTH, :]))
    h1_ref[...] = h1
    hn = _rms_scale(h1) * n2w_ref[...]
    nt = (((1,), (1,)), ((), ()))
    lt_ref[...] = lax.dot_general(wr_ref[...], hn, nt, preferred_element_type=F32,
                                  precision=HIGHEST) + rb_ref[...]
    half = D_MODEL // 2
    hp_ref[...] = _pack_bf16_pair(hn[:, :half], hn[:, half:])


def _outproj_call(yg, ys, x2d, wo, n2w, wr_t, rb, tm):
    n_tok, d = x2d.shape
    return pl.pallas_call(
        _outproj_kernel,
        grid=(n_tok // tm,),
        in_specs=[
            pl.BlockSpec((tm, GDN_WIDTH), lambda i: (i, 0)),
            pl.BlockSpec((tm, SC_WIDTH), lambda i: (i, 0)),
            pl.BlockSpec((tm, d), lambda i: (i, 0)),
            _resident((GDN_WIDTH + SC_WIDTH, d)),
            _resident((1, d)),
            _resident((LANES, d)),
            _resident((LANES, 1)),
        ],
        out_specs=[
            pl.BlockSpec((tm, d), lambda i: (i, 0)),
            pl.BlockSpec((tm, d // 2), lambda i: (i, 0)),
            pl.BlockSpec((LANES, tm), lambda i: (0, i)),
        ],
        out_shape=[
            jax.ShapeDtypeStruct((n_tok, d), F32),
            jax.ShapeDtypeStruct((n_tok, d // 2), U32),
            jax.ShapeDtypeStruct((LANES, n_tok), F32),
        ],
        compiler_params=_params("arbitrary"),
        name="out_proj_router",
    )(yg, ys, x2d, wo, n2w, wr_t, rb)


def _first_argmax(vals, iota, size):
    m = jnp.max(vals, axis=0, keepdims=True)
    idx = jnp.min(jnp.where(vals == m, iota, size), axis=0, keepdims=True)
    return m, idx


def _route_kernel(lt_ref, idx_ref, gate_ref, cnt_ref, carry_sc, *, tr):
    @pl.when(pl.program_id(0) == 0)
    def _():
        carry_sc[...] = jnp.zeros_like(carry_sc)

    iota8 = lax.broadcasted_iota(I32, (EXPERTS_PER_GROUP, tr), 0)
    gl = lt_ref[0:N_GROUPS, :]
    gmax, gidx = _first_argmax(gl, iota8, N_GROUPS)
    g_w = 1.0 / jnp.sum(jnp.exp(gl - gmax), axis=0, keepdims=True)

    ig = jnp.zeros((EXPERTS_PER_GROUP, tr), F32)
    for g in range(N_GROUPS):
        lo = N_GROUPS + g * EXPERTS_PER_GROUP
        ig = jnp.where(gidx == g, lt_ref[lo:lo + EXPERTS_PER_GROUP, :], ig)
    pe = jnp.exp(ig - jnp.max(ig, axis=0, keepdims=True))
    p = pe / jnp.sum(pe, axis=0, keepdims=True)
    p1, i1 = _first_argmax(p, iota8, EXPERTS_PER_GROUP)
    p2, i2 = _first_argmax(jnp.where(iota8 == i1, -1.0, p), iota8, EXPERTS_PER_GROUP)
    denom = p1 + p2
    e1 = gidx * EXPERTS_PER_GROUP + i1
    e2 = gidx * EXPERTS_PER_GROUP + i2

    iota_e = lax.broadcasted_iota(I32, (N_EXPERTS, tr), 0)
    oh1 = (iota_e == e1).astype(F32)
    oh2 = (iota_e == e2).astype(F32)
    ohs = oh1 + oh2
    r = lax.broadcasted_iota(I32, (tr, tr), 0)
    c = lax.broadcasted_iota(I32, (tr, tr), 1)
    before = jnp.where(r < c, 1.0, 0.0).astype(BF16)
    base = carry_sc[:, 0:1] + _dot(ohs.astype(BF16), before)
    rank1 = jnp.sum(oh1 * base, axis=0, keepdims=True)
    rank2 = jnp.sum(oh2 * base, axis=0, keepdims=True)
    carry_sc[...] = carry_sc[...] + jnp.sum(ohs, axis=1, keepdims=True)
    cnt_ref[...] = carry_sc[...]

    idx_ref[...] = jnp.zeros_like(idx_ref)
    idx_ref[0:1, :] = e1
    idx_ref[1:2, :] = e2
    idx_ref[2:3, :] = rank1.astype(I32)
    idx_ref[3:4, :] = rank2.astype(I32)
    gate_ref[...] = jnp.zeros_like(gate_ref)
    gate_ref[0:1, :] = g_w * p1 / denom
    gate_ref[1:2, :] = g_w * p2 / denom


def _route_call(lt, tr):
    n_tok = lt.shape[1]
    return pl.pallas_call(
        functools.partial(_route_kernel, tr=tr),
        grid=(n_tok // tr,),
        in_specs=[pl.BlockSpec((LANES, tr), lambda i: (0, i))],
        out_specs=[
            pl.BlockSpec((SUBLANES, tr), lambda i: (0, i)),
            pl.BlockSpec((SUBLANES, tr), lambda i: (0, i)),
            pl.BlockSpec((N_EXPERTS, LANES), lambda i: (0, 0)),
        ],
        out_shape=[
            jax.ShapeDtypeStruct((SUBLANES, n_tok), I32),
            jax.ShapeDtypeStruct((SUBLANES, n_tok), F32),
            jax.ShapeDtypeStruct((N_EXPERTS, LANES), F32),
        ],
        scratch_shapes=[pltpu.VMEM((N_EXPERTS, LANES), F32)],
        compiler_params=_params("arbitrary"),
        name="moe_route",
    )(lt)


def _dispatch_kernel(d1_ref, d2_ref, hp_ref, xs_in_ref, xs_ref, sem, *, tb):
    del xs_in_ref
    copies = []
    for u in range(tb):
        src = hp_ref.at[pl.ds(u, 1)]
        for d_ref in (d1_ref, d2_ref):
            cp = pltpu.make_async_copy(src, xs_ref.at[pl.ds(d_ref[0, 0, u], 1)], sem)
            cp.start()
            copies.append(cp)
    for cp in copies:
        cp.wait()


def _dispatch_call(d1, d2, hp, n_rows, tb):
    n_tok, width = hp.shape
    xs0 = jnp.zeros((n_rows, width), U32)
    smem = pl.BlockSpec((1, 1, tb), lambda i: (i, 0, 0), memory_space=pltpu.SMEM)
    return pl.pallas_call(
        functools.partial(_dispatch_kernel, tb=tb),
        grid=(n_tok // tb,),
        in_specs=[
            smem, smem,
            pl.BlockSpec((tb, width), lambda i: (i, 0)),
            pl.BlockSpec(memory_space=pl.ANY),
        ],
        out_specs=pl.BlockSpec(memory_space=pl.ANY),
        out_shape=jax.ShapeDtypeStruct((n_rows, width), U32),
        scratch_shapes=[pltpu.SemaphoreType.DMA(())],
        input_output_aliases={3: 0},
        compiler_params=_params("arbitrary"),
        name="moe_dispatch",
    )(d1.reshape(n_tok // tb, 1, tb), d2.reshape(n_tok // tb, 1, tb), hp, xs0)


def _expert_kernel(be_ref, nu_ref, xs_ref, wg_ref, wu_ref, wd_ref, ys_ref):
    del be_ref
    used = pl.program_id(0) < nu_ref[0]

    @pl.when(used)
    def _():
        hi, lo = _unpack_bf16_pair(xs_ref[...])
        xb = jnp.concatenate([hi, lo], axis=1)
        hid = _silu(_dot(xb, wg_ref[...])) * _dot(xb, wu_ref[...])
        ys_ref[...] = _dot(hid.astype(BF16), wd_ref[...])

    @pl.when(jnp.logical_not(used))
    def _():
        ys_ref[...] = jnp.zeros_like(ys_ref)


def _expert_call(block_expert, n_used, xs, wg, wu, wd, bm):
    n_rows, width = xs.shape
    d = 2 * width
    nb = n_rows // bm
    grid_spec = pltpu.PrefetchScalarGridSpec(
        num_scalar_prefetch=2,
        grid=(nb,),
        in_specs=[
            pl.BlockSpec((bm, width), lambda i, be, nu: (jnp.minimum(i, nu[0] - 1), 0)),
            pl.BlockSpec((None, d, D_EXPERT), lambda i, be, nu: (be[i], 0, 0)),
            pl.BlockSpec((None, d, D_EXPERT), lambda i, be, nu: (be[i], 0, 0)),
            pl.BlockSpec((None, D_EXPERT, d), lambda i, be, nu: (be[i], 0, 0)),
        ],
        out_specs=pl.BlockSpec((bm, d), lambda i, be, nu: (i, 0)),
    )
    return pl.pallas_call(
        _expert_kernel,
        grid_spec=grid_spec,
        out_shape=jax.ShapeDtypeStruct((n_rows, d), F32),
        compiler_params=_params("arbitrary"),
        name="moe_experts",
    )(block_expert, n_used, xs, wg, wu, wd)


def _combine_kernel(d1_ref, d2_ref, gate_ref, h1_ref, wf_ref, ys_ref, out_ref,
                    buf1, buf2, sem, *, tc):
    copies = []
    for u in range(tc):
        for d_ref, buf in ((d1_ref, buf1), (d2_ref, buf2)):
            cp = pltpu.make_async_copy(ys_ref.at[pl.ds(d_ref[0, 0, u], 1)],
                                       buf.at[pl.ds(u, 1)], sem)
            cp.start()
            copies.append(cp)
    for cp in copies:
        cp.wait()
    gates = gate_ref[...]
    h = h1_ref[...] + buf1[...] * gates[:, 0:1] + buf2[...] * gates[:, 1:2]
    out_ref[...] = _rms_scale(h) * wf_ref[...]


def _combine_call(d1, d2, gates, h1, wf, ys, tc):
    n_tok, d = h1.shape
    smem = pl.BlockSpec((1, 1, tc), lambda i: (i, 0, 0), memory_space=pltpu.SMEM)
    return pl.pallas_call(
        functools.partial(_combine_kernel, tc=tc),
        grid=(n_tok // tc,),
        in_specs=[
            smem, smem,
            pl.BlockSpec((tc, LANES), lambda i: (i, 0)),
            pl.BlockSpec((tc, d), lambda i: (i, 0)),
            _resident((1, d)),
            pl.BlockSpec(memory_space=pl.ANY),
        ],
        out_specs=pl.BlockSpec((tc, d), lambda i: (i, 0)),
        out_shape=jax.ShapeDtypeStruct((n_tok, d), F32),
        scratch_shapes=[
            pltpu.VMEM((tc, d), F32),
            pltpu.VMEM((tc, d), F32),
            pltpu.SemaphoreType.DMA(()),
        ],
        compiler_params=_params("arbitrary"),
        name="moe_combine",
    )(d1.reshape(n_tok // tc, 1, tc), d2.reshape(n_tok // tc, 1, tc), gates, h1, wf, ys)


def _tiles(seq, n_tok):
    return dict(
        proj=min(512, seq),
        gdn=min(512, seq),
        outp=min(512, n_tok),
        route=min(512, n_tok),
        disp=min(128, n_tok),
        bm=256,
        comb=min(128, n_tok),
    )


def _layer(h, n1w, w_in, conv_qkv_w, a_log, dt_bias, gdn_norm_w, conv_sc_w, w_out,
           n2w, wg_r, bg_r, we_r, be_r, w_gate, w_up, w_down, final_w, last):
    bsz, seq, d = h.shape
    n_tok = bsz * seq
    ts = _tiles(seq, n_tok)

    qkvz_cols = 4 * GDN_WIDTH
    w_qkvz = w_in[:, :qkvz_cols].astype(BF16)
    w_ba = jnp.pad(w_in[:, qkvz_cols:qkvz_cols + 2 * GDN_HEADS],
                   ((0, 0), (0, LANES - 2 * GDN_HEADS))).astype(BF16)
    sc0 = qkvz_cols + 2 * GDN_HEADS
    w_sc = w_in[:, sc0:].reshape(d, 3, SC_WIDTH // SC_TILE, SC_TILE)
    w_sc = jnp.swapaxes(w_sc, 1, 2).reshape(d, 3 * SC_WIDTH).astype(BF16)
    pad8 = (GDN_HEADS, LANES - 2 * GDN_HEADS)
    alog_v = jnp.pad(a_log, pad8).reshape(1, LANES)
    dtb_v = jnp.pad(dt_bias, pad8).reshape(1, LANES)

    qkvz, bg = _qkvz_call(h, n1w.reshape(1, d), w_qkvz, w_ba, conv_qkv_w, alog_v, dtb_v,
                          ts["proj"])
    y_sc = _sc_call(h, n1w.reshape(1, d), w_sc, conv_sc_w, ts["proj"])

    gt = jnp.swapaxes(bg[:, :, :2 * GDN_HEADS].reshape(bsz, seq // CHUNK, CHUNK, 2 * GDN_HEADS),
                      2, 3)
    y_gdn = _gdn_call(qkvz, bg, gt, gdn_norm_w.reshape(1, HEAD_DIM), ts["gdn"])

    n_router = N_GROUPS + N_EXPERTS
    wr_t = jnp.pad(jnp.concatenate([wg_r, we_r], axis=1).T, ((0, LANES - n_router), (0, 0)))
    rb = jnp.pad(jnp.concatenate([bg_r, be_r]), (0, LANES - n_router)).reshape(LANES, 1)
    x2d = h.reshape(n_tok, d)
    h1, hp, lt = _outproj_call(y_gdn.reshape(n_tok, GDN_WIDTH), y_sc.reshape(n_tok, SC_WIDTH),
                               x2d, w_out.astype(BF16), n2w.reshape(1, d), wr_t, rb, ts["outp"])

    idx, gates_t, cnt = _route_call(lt, ts["route"])
    bm = ts["bm"]
    counts = cnt[:, 0].astype(I32)
    padded = (counts + bm - 1) // bm * bm
    pad_end = jnp.cumsum(padded)
    pad_start = pad_end - padded
    d1 = pad_start[idx[0]] + idx[2]
    d2 = pad_start[idx[1]] + idx[3]
    nb = n_tok * TOP_K // bm + N_EXPERTS
    block_expert = jnp.minimum(
        jnp.searchsorted(pad_end, jnp.arange(nb, dtype=I32) * bm, side="right"),
        N_EXPERTS - 1).astype(I32)
    n_used = (pad_end[-1:] // bm).astype(I32)
    gates = jnp.pad(gates_t[:TOP_K].T, ((0, 0), (0, LANES - TOP_K)))

    xs = _dispatch_call(d1, d2, hp, nb * bm, ts["disp"])
    ys = _expert_call(block_expert, n_used, xs, w_gate.astype(BF16), w_up.astype(BF16),
                      w_down.astype(BF16), bm)
    out = _combine_call(d1, d2, gates, h1, final_w.reshape(1, d), ys, ts["comb"])
    del last
    return out.reshape(bsz, seq, d)


def kernel(x, norm1_w, w_in, conv_qkv_w, a_log, dt_bias, gdn_norm_w, conv_sc_w, w_out, norm2_w,
           router_group_w, router_group_b, router_expert_w, router_expert_b, w_gate, w_up,
           w_down, final_norm_w):
    depth = norm1_w.shape[0]
    assert depth == 1, "the combine call applies the final RMSNorm, so a single layer is assumed"
    return _layer(x, norm1_w[0], w_in[0], conv_qkv_w[0], a_log[0], dt_bias[0], gdn_norm_w[0],
                  conv_sc_w[0], w_out[0], norm2_w[0], router_group_w[0], router_group_b[0],
                  router_expert_w[0], router_expert_b[0], w_gate[0], w_up[0], w_down[0],
                  final_norm_w, True)
```

```python
import functools

import jax
import jax.numpy as jnp
from jax import lax
from jax.experimental import pallas as pl
from jax.experimental.pallas import tpu as pltpu

F32 = jnp.float32
BF16 = jnp.bfloat16
I32 = jnp.int32
U32 = jnp.uint32
HIGHEST = lax.Precision.HIGHEST

EPS = 1e-6
D_MODEL = 2048
HEAD_DIM = 128
GDN_HEADS = 8
GDN_WIDTH = GDN_HEADS * HEAD_DIM
GDN_CONV = 4
CHUNK = 64
SC_WIDTH = 1024
SC_CONV = 3
N_GROUPS = 8
EXPERTS_PER_GROUP = 8
N_EXPERTS = N_GROUPS * EXPERTS_PER_GROUP
TOP_K = 2
D_EXPERT = 512
LANES = 128
SUBLANES = 8
SC_TILE = 256
VMEM_LIMIT = 56 * 1024 * 1024


def _sigmoid(x):
    return 1.0 / (1.0 + jnp.exp(-x))


def _silu(x):
    return x * _sigmoid(x)


def _softplus(x):
    return jnp.maximum(x, 0.0) + jnp.log1p(jnp.exp(-jnp.abs(x)))


def _rms_scale(x):
    return x * lax.rsqrt(jnp.mean(x * x, axis=-1, keepdims=True) + EPS)


def _dot(a, b):
    return jnp.dot(a, b, preferred_element_type=F32)


def _hdot(a, b):
    return jnp.dot(a, b, preferred_element_type=F32, precision=HIGHEST)


def _params(*sem):
    return pltpu.CompilerParams(dimension_semantics=sem, vmem_limit_bytes=VMEM_LIMIT)


def _resident(shape):
    return pl.BlockSpec(shape, lambda *_: (0,) * len(shape), pipeline_mode=pl.Buffered(1))


def _qkvz_kernel(x_ref, n1w_ref, w_ref, wba_ref, cw_ref, alog_ref, dtb_ref, ltri_ref,
                 out_ref, bg_ref, buf_sc, carry_sc, *, tm):
    @pl.when(pl.program_id(1) == 0)
    def _():
        carry_sc[...] = jnp.zeros_like(carry_sc)

    hn = (_rms_scale(x_ref[...]) * n1w_ref[...]).astype(BF16)

    ba = _dot(hn, wba_ref[...])
    lane = lax.broadcasted_iota(I32, ba.shape, 1)
    g = -jnp.exp(alog_ref[...]) * _softplus(ba + dtb_ref[...])
    g = jnp.where((lane >= GDN_HEADS) & (lane < 2 * GDN_HEADS), g, 0.0)
    gc = _hdot(ltri_ref[...], g)
    bg_ref[...] = jnp.where(lane < GDN_HEADS, _sigmoid(ba), gc)

    for j in range(4):
        cols = slice(j * GDN_WIDTH, (j + 1) * GDN_WIDTH)
        p = _dot(hn, w_ref[:, cols])
        if j == 3:
            out_ref[:, cols] = p.astype(BF16)
            continue
        buf_sc[0:SUBLANES, :] = carry_sc[j]
        buf_sc[SUBLANES:SUBLANES + tm, :] = p
        carry_sc[j] = p[tm - SUBLANES:tm, :]
        cw = cw_ref[:, cols]
        y = cw[GDN_CONV - 1:GDN_CONV, :] * p
        for s in range(1, GDN_CONV):
            y = y + cw[GDN_CONV - 1 - s:GDN_CONV - s, :] * buf_sc[pl.ds(SUBLANES - s, tm), :]
        y = _silu(y)
        if j == 2:
            out_ref[:, cols] = y.astype(BF16)
            continue
        scale = HEAD_DIM ** -0.5 if j == 0 else 1.0
        for h in range(GDN_HEADS):
            hc = slice(h * HEAD_DIM, (h + 1) * HEAD_DIM)
            yh = y[:, hc]
            yh = yh * (lax.rsqrt(jnp.sum(yh * yh, axis=-1, keepdims=True) + EPS) * scale)
            out_ref[:, j * GDN_WIDTH + h * HEAD_DIM:j * GDN_WIDTH + (h + 1) * HEAD_DIM] = (
                yh.astype(BF16))


def _qkvz_call(x, n1w, w, wba, cw, alog, dtb, tm):
    bsz, seq, d = x.shape
    n = w.shape[1]
    idx = jnp.arange(tm)
    ltri = ((idx[:, None] >= idx[None, :]) &
            (idx[:, None] // CHUNK == idx[None, :] // CHUNK)).astype(F32)
    return pl.pallas_call(
        functools.partial(_qkvz_kernel, tm=tm),
        grid=(bsz, seq // tm),
        in_specs=[
            pl.BlockSpec((None, tm, d), lambda b, i: (b, i, 0)),
            _resident((1, d)),
            _resident((d, n)),
            _resident((d, LANES)),
            _resident((GDN_CONV, 3 * GDN_WIDTH)),
            _resident((1, LANES)),
            _resident((1, LANES)),
            _resident((tm, tm)),
        ],
        out_specs=[
            pl.BlockSpec((None, tm, n), lambda b, i: (b, i, 0)),
            pl.BlockSpec((None, tm, LANES), lambda b, i: (b, i, 0)),
        ],
        out_shape=[
            jax.ShapeDtypeStruct((bsz, seq, n), BF16),
            jax.ShapeDtypeStruct((bsz, seq, LANES), F32),
        ],
        scratch_shapes=[
            pltpu.VMEM((tm + SUBLANES, GDN_WIDTH), F32),
            pltpu.VMEM((3, SUBLANES, GDN_WIDTH), F32),
        ],
        compiler_params=_params("arbitrary", "arbitrary"),
        name="qkvz_proj",
    )(x, n1w, w, wba, cw, alog, dtb, ltri)


def _sc_kernel(x_ref, n1w_ref, w_ref, cw_ref, out_ref, buf_sc, carry_sc, *, tm):
    @pl.when(pl.program_id(1) == 0)
    def _():
        carry_sc[...] = jnp.zeros_like(carry_sc)

    hn = (_rms_scale(x_ref[...]) * n1w_ref[...]).astype(BF16)
    for t in range(SC_WIDTH // SC_TILE):
        p = _dot(hn, w_ref[:, 3 * SC_TILE * t:3 * SC_TILE * (t + 1)])
        gate_b = p[:, 0:SC_TILE]
        ch = p[:, SC_TILE:2 * SC_TILE] * p[:, 2 * SC_TILE:3 * SC_TILE]
        buf_sc[0:SUBLANES, :] = carry_sc[t]
        buf_sc[SUBLANES:SUBLANES + tm, :] = ch
        carry_sc[t] = ch[tm - SUBLANES:tm, :]
        cw = cw_ref[:, SC_TILE * t:SC_TILE * (t + 1)]
        y = cw[SC_CONV - 1:SC_CONV, :] * ch
        for s in range(1, SC_CONV):
            y = y + cw[SC_CONV - 1 - s:SC_CONV - s, :] * buf_sc[pl.ds(SUBLANES - s, tm), :]
        out_ref[:, SC_TILE * t:SC_TILE * (t + 1)] = (gate_b * y).astype(BF16)


def _sc_call(x, n1w, w, cw, tm):
    bsz, seq, d = x.shape
    return pl.pallas_call(
        functools.partial(_sc_kernel, tm=tm),
        grid=(bsz, seq // tm),
        in_specs=[
            pl.BlockSpec((None, tm, d), lambda b, i: (b, i, 0)),
            _resident((1, d)),
            _resident((d, 3 * SC_WIDTH)),
            _resident((SC_CONV, SC_WIDTH)),
        ],
        out_specs=pl.BlockSpec((None, tm, SC_WIDTH), lambda b, i: (b, i, 0)),
        out_shape=jax.ShapeDtypeStruct((bsz, seq, SC_WIDTH), BF16),
        scratch_shapes=[
            pltpu.VMEM((tm + SUBLANES, SC_TILE), F32),
            pltpu.VMEM((SC_WIDTH // SC_TILE, SUBLANES, SC_TILE), F32),
        ],
        compiler_params=_params("arbitrary", "arbitrary"),
        name="shortconv_proj",
    )(x, n1w, w, cw)


SOLVE_PASSES = 1


def _split_bf16(x):
    hi = x.astype(BF16)
    return hi, (x - hi.astype(F32)).astype(BF16)


def _sdot(a, b):
    if SOLVE_PASSES == 1:
        return _dot(a.astype(BF16), b.astype(BF16))
    ah, al = _split_bf16(a)
    bh, bl = _split_bf16(b)
    return _dot(ah, bh) + _dot(ah, bl) + _dot(al, bh)


def _gdn_kernel(q_ref, k_ref, v_ref, z_ref, bg_ref, gt_ref, nw_ref, out_ref, state_sc, *, tc):
    @pl.when(pl.program_id(1) == 0)
    def _():
        state_sc[...] = jnp.zeros_like(state_sc)

    row = lax.broadcasted_iota(I32, (CHUNK, CHUNK), 0)
    col = lax.broadcasted_iota(I32, (CHUNK, CHUNK), 1)
    causal = row >= col
    strict = row > col
    eye = (row == col).astype(F32)
    same16 = (row // 16) == (col // 16)
    same32 = (row // 32) == (col // 32)
    m16 = same16.astype(F32)
    m32 = (same32 & jnp.logical_not(same16)).astype(F32)
    m64 = jnp.logical_not(same32).astype(F32)
    nw = nw_ref[...]
    tt = (((0,), (0,)), ((), ()))
    nt = (((1,), (1,)), ((), ()))
    heads = range(GDN_HEADS)

    def hcols(h):
        return slice(h * HEAD_DIM, (h + 1) * HEAD_DIM)

    def chunk_body(c, carry):
        r0 = pl.multiple_of(c * CHUNK, CHUNK)
        rows = pl.ds(r0, CHUNK)
        bgc = bg_ref[rows, :]
        gt = gt_ref[c]
        q = [q_ref[rows, hcols(h)] for h in heads]
        k = [k_ref[rows, hcols(h)] for h in heads]
        beta = [bgc[:, h:h + 1] for h in heads]
        gc = [bgc[:, GDN_HEADS + h:GDN_HEADS + h + 1] for h in heads]
        glast = [g[CHUNK - 1:CHUNK, :] for g in gc]

        kq = [lax.dot_general(jnp.concatenate([k[h], q[h]], axis=0), k[h], nt,
                              preferred_element_type=F32) for h in heads]
        a, attn = [], []
        for h in heads:
            gcr = gt[GDN_HEADS + h:GDN_HEADS + h + 1, :]
            decay = jnp.where(causal, jnp.exp(gc[h] - gcr), 0.0)
            a.append(jnp.where(strict, kq[h][:CHUNK] * beta[h] * decay, 0.0))
            attn.append((kq[h][CHUNK:] * decay).astype(BF16))

        n = [-(a[h] * m16) for h in heads]
        p = [eye + n[h] for h in heads]
        for _ in range(3):
            n = [_sdot(n[h], n[h]) for h in heads]
            p = [p[h] + _sdot(p[h], n[h]) for h in heads]
        for m in (m32, m64):
            e = [_sdot(a[h] * m, p[h]) for h in heads]
            p = [p[h] - _sdot(p[h], e[h]) for h in heads]

        eg = [jnp.exp(g) for g in gc]
        kf = [k[h].astype(F32) for h in heads]
        uw = []
        for h in heads:
            v = v_ref[rows, hcols(h)].astype(F32)
            rhs = jnp.concatenate([v * beta[h], kf[h] * (beta[h] * eg[h])], axis=1)
            uw.append(_sdot(p[h], rhs))

        state = [state_sc[h] for h in heads]
        rs = []
        for h in heads:
            wq = jnp.concatenate([uw[h][:, HEAD_DIM:], q[h].astype(F32) * eg[h]], axis=0)
            rs.append(_dot(wq.astype(BF16), state[h].astype(BF16)))
        vb = [(uw[h][:, :HEAD_DIM] - rs[h][:CHUNK]).astype(BF16) for h in heads]
        o = [rs[h][CHUNK:] + _dot(attn[h], vb[h]) for h in heads]
        for h in heads:
            k_dec = (kf[h] * jnp.exp(glast[h] - gc[h])).astype(BF16)
            state_sc[h] = state[h] * jnp.exp(glast[h]) + lax.dot_general(
                k_dec, vb[h], tt, preferred_element_type=F32)
        for h in heads:
            z = z_ref[rows, hcols(h)].astype(F32)
            out_ref[rows, hcols(h)] = (_rms_scale(o[h]) * nw * _silu(z)).astype(BF16)
        return carry

    lax.fori_loop(0, tc // CHUNK, chunk_body, 0)


def _gdn_call(qkvz, bg, gt, nw, tc):
    bsz, seq, _ = qkvz.shape

    def col(j):
        return pl.BlockSpec((None, tc, GDN_WIDTH), lambda b, i: (b, i, j))

    return pl.pallas_call(
        functools.partial(_gdn_kernel, tc=tc),
        grid=(bsz, seq // tc),
        in_specs=[
            col(0), col(1), col(2), col(3),
            pl.BlockSpec((None, tc, LANES), lambda b, i: (b, i, 0)),
            pl.BlockSpec((None, tc // CHUNK, 2 * GDN_HEADS, CHUNK), lambda b, i: (b, i, 0, 0)),
            _resident((1, HEAD_DIM)),
        ],
        out_specs=pl.BlockSpec((None, tc, GDN_WIDTH), lambda b, i: (b, i, 0)),
        out_shape=jax.ShapeDtypeStruct((bsz, seq, GDN_WIDTH), BF16),
        scratch_shapes=[pltpu.VMEM((GDN_HEADS, HEAD_DIM, HEAD_DIM), F32)],
        compiler_params=_params("arbitrary", "arbitrary"),
        name="gated_delta_rule",
    )(qkvz, qkvz, qkvz, qkvz, bg, gt, nw)


def _pack_bf16_pair(hi, lo):
    hb = lax.bitcast_convert_type(hi.astype(BF16).astype(F32), U32)
    lb = lax.bitcast_convert_type(lo.astype(BF16).astype(F32), U32)
    return (hb & jnp.uint32(0xFFFF0000)) | lax.shift_right_logical(lb, jnp.uint32(16))


def _unpack_bf16_pair(p):
    hi = lax.bitcast_convert_type(p & jnp.uint32(0xFFFF0000), F32).astype(BF16)
    lo = lax.bitcast_convert_type(lax.shift_left(p, jnp.uint32(16)), F32).astype(BF16)
    return hi, lo


def _outproj_kernel(yg_ref, ys_ref, x_ref, wo_ref, n2w_ref, wr_ref, rb_ref,
                    h1_ref, hp_ref, lt_ref):
    h1 = (x_ref[...] + _dot(yg_ref[...], wo_ref[0:GDN_WIDTH, :])
          + _dot(ys_ref[...], wo_ref[GDN_WIDTH:GDN_WIDTH + SC_WIDTH, :]))
    h1_ref[...] = h1
    hn = _rms_scale(h1) * n2w_ref[...]
    nt = (((1,), (1,)), ((), ()))
    lt_ref[...] = lax.dot_general(wr_ref[...], hn, nt, preferred_element_type=F32,
                                  precision=HIGHEST) + rb_ref[...]
    half = D_MODEL // 2
    hp_ref[...] = _pack_bf16_pair(hn[:, :half], hn[:, half:])


def _outproj_call(yg, ys, x2d, wo, n2w, wr_t, rb, tm):
    n_tok, d = x2d.shape
    return pl.pallas_call(
        _outproj_kernel,
        grid=(n_tok // tm,),
        in_specs=[
            pl.BlockSpec((tm, GDN_WIDTH), lambda i: (i, 0)),
            pl.BlockSpec((tm, SC_WIDTH), lambda i: (i, 0)),
            pl.BlockSpec((tm, d), lambda i: (i, 0)),
            _resident((GDN_WIDTH + SC_WIDTH, d)),
            _resident((1, d)),
            _resident((LANES, d)),
            _resident((LANES, 1)),
        ],
        out_specs=[
            pl.BlockSpec((tm, d), lambda i: (i, 0)),
            pl.BlockSpec((tm, d // 2), lambda i: (i, 0)),
            pl.BlockSpec((LANES, tm), lambda i: (0, i)),
        ],
        out_shape=[
            jax.ShapeDtypeStruct((n_tok, d), F32),
            jax.ShapeDtypeStruct((n_tok, d // 2), U32),
            jax.ShapeDtypeStruct((LANES, n_tok), F32),
        ],
        compiler_params=_params("arbitrary"),
        name="out_proj_router",
    )(yg, ys, x2d, wo, n2w, wr_t, rb)


def _first_argmax(vals, iota, size):
    m = jnp.max(vals, axis=0, keepdims=True)
    idx = jnp.min(jnp.where(vals == m, iota, size), axis=0, keepdims=True)
    return m, idx


def _route_kernel(lt_ref, idx_ref, gate_ref, cnt_ref, carry_sc, *, tr):
    @pl.when(pl.program_id(0) == 0)
    def _():
        carry_sc[...] = jnp.zeros_like(carry_sc)

    iota8 = lax.broadcasted_iota(I32, (EXPERTS_PER_GROUP, tr), 0)
    gl = lt_ref[0:N_GROUPS, :]
    gmax, gidx = _first_argmax(gl, iota8, N_GROUPS)
    g_w = 1.0 / jnp.sum(jnp.exp(gl - gmax), axis=0, keepdims=True)

    ig = jnp.zeros((EXPERTS_PER_GROUP, tr), F32)
    for g in range(N_GROUPS):
        lo = N_GROUPS + g * EXPERTS_PER_GROUP
        ig = jnp.where(gidx == g, lt_ref[lo:lo + EXPERTS_PER_GROUP, :], ig)
    pe = jnp.exp(ig - jnp.max(ig, axis=0, keepdims=True))
    p = pe / jnp.sum(pe, axis=0, keepdims=True)
    p1, i1 = _first_argmax(p, iota8, EXPERTS_PER_GROUP)
    p2, i2 = _first_argmax(jnp.where(iota8 == i1, -1.0, p), iota8, EXPERTS_PER_GROUP)
    denom = p1 + p2
    e1 = gidx * EXPERTS_PER_GROUP + i1
    e2 = gidx * EXPERTS_PER_GROUP + i2

    iota_e = lax.broadcasted_iota(I32, (N_EXPERTS, tr), 0)
    oh1 = (iota_e == e1).astype(F32)
    oh2 = (iota_e == e2).astype(F32)
    ohs = oh1 + oh2
    r = lax.broadcasted_iota(I32, (tr, tr), 0)
    c = lax.broadcasted_iota(I32, (tr, tr), 1)
    before = jnp.where(r < c, 1.0, 0.0).astype(BF16)
    base = carry_sc[:, 0:1] + _dot(ohs.astype(BF16), before)
    rank1 = jnp.sum(oh1 * base, axis=0, keepdims=True)
    rank2 = jnp.sum(oh2 * base, axis=0, keepdims=True)
    carry_sc[...] = carry_sc[...] + jnp.sum(ohs, axis=1, keepdims=True)
    cnt_ref[...] = carry_sc[...]

    idx_ref[...] = jnp.zeros_like(idx_ref)
    idx_ref[0:1, :] = e1
    idx_ref[1:2, :] = e2
    idx_ref[2:3, :] = rank1.astype(I32)
    idx_ref[3:4, :] = rank2.astype(I32)
    gate_ref[...] = jnp.zeros_like(gate_ref)
    gate_ref[0:1, :] = g_w * p1 / denom
    gate_ref[1:2, :] = g_w * p2 / denom


def _route_call(lt, tr):
    n_tok = lt.shape[1]
    return pl.pallas_call(
        functools.partial(_route_kernel, tr=tr),
        grid=(n_tok // tr,),
        in_specs=[pl.BlockSpec((LANES, tr), lambda i: (0, i))],
        out_specs=[
            pl.BlockSpec((SUBLANES, tr), lambda i: (0, i)),
            pl.BlockSpec((SUBLANES, tr), lambda i: (0, i)),
            pl.BlockSpec((N_EXPERTS, LANES), lambda i: (0, 0)),
        ],
        out_shape=[
            jax.ShapeDtypeStruct((SUBLANES, n_tok), I32),
            jax.ShapeDtypeStruct((SUBLANES, n_tok), F32),
            jax.ShapeDtypeStruct((N_EXPERTS, LANES), F32),
        ],
        scratch_shapes=[pltpu.VMEM((N_EXPERTS, LANES), F32)],
        compiler_params=_params("arbitrary"),
        name="moe_route",
    )(lt)


def _dispatch_kernel(d1_ref, d2_ref, hp_ref, xs_in_ref, xs_ref, sem, *, tb):
    del xs_in_ref
    copies = []
    for u in range(tb):
        src = hp_ref.at[pl.ds(u, 1)]
        for d_ref in (d1_ref, d2_ref):
            cp = pltpu.make_async_copy(src, xs_ref.at[pl.ds(d_ref[0, 0, u], 1)], sem)
            cp.start()
            copies.append(cp)
    for cp in copies:
        cp.wait()


def _dispatch_call(d1, d2, hp, n_rows, tb):
    n_tok, width = hp.shape
    xs0 = jnp.zeros((n_rows, width), U32)
    smem = pl.BlockSpec((1, 1, tb), lambda i: (i, 0, 0), memory_space=pltpu.SMEM)
    return pl.pallas_call(
        functools.partial(_dispatch_kernel, tb=tb),
        grid=(n_tok // tb,),
        in_specs=[
            smem, smem,
            pl.BlockSpec((tb, width), lambda i: (i, 0)),
            pl.BlockSpec(memory_space=pl.ANY),
        ],
        out_specs=pl.BlockSpec(memory_space=pl.ANY),
        out_shape=jax.ShapeDtypeStruct((n_rows, width), U32),
        scratch_shapes=[pltpu.SemaphoreType.DMA(())],
        input_output_aliases={3: 0},
        compiler_params=_params("arbitrary"),
        name="moe_dispatch",
    )(d1.reshape(n_tok // tb, 1, tb), d2.reshape(n_tok // tb, 1, tb), hp, xs0)


def _expert_kernel(be_ref, nu_ref, xs_ref, wg_ref, wu_ref, wd_ref, ys_ref):
    del be_ref
    used = pl.program_id(0) < nu_ref[0]

    @pl.when(used)
    def _():
        hi, lo = _unpack_bf16_pair(xs_ref[...])
        xb = jnp.concatenate([hi, lo], axis=1)
        hid = _silu(_dot(xb, wg_ref[...])) * _dot(xb, wu_ref[...])
        ys_ref[...] = _dot(hid.astype(BF16), wd_ref[...])

    @pl.when(jnp.logical_not(used))
    def _():
        ys_ref[...] = jnp.zeros_like(ys_ref)


def _expert_call(block_expert, n_used, xs, wg, wu, wd, bm):
    n_rows, width = xs.shape
    d = 2 * width
    nb = n_rows // bm
    grid_spec = pltpu.PrefetchScalarGridSpec(
        num_scalar_prefetch=2,
        grid=(nb,),
        in_specs=[
            pl.BlockSpec((bm, width), lambda i, be, nu: (jnp.minimum(i, nu[0] - 1), 0)),
            pl.BlockSpec((None, d, D_EXPERT), lambda i, be, nu: (be[i], 0, 0)),
            pl.BlockSpec((None, d, D_EXPERT), lambda i, be, nu: (be[i], 0, 0)),
            pl.BlockSpec((None, D_EXPERT, d), lambda i, be, nu: (be[i], 0, 0)),
        ],
        out_specs=pl.BlockSpec((bm, d), lambda i, be, nu: (i, 0)),
    )
    return pl.pallas_call(
        _expert_kernel,
        grid_spec=grid_spec,
        out_shape=jax.ShapeDtypeStruct((n_rows, d), F32),
        compiler_params=_params("arbitrary"),
        name="moe_experts",
    )(block_expert, n_used, xs, wg, wu, wd)


def _combine_kernel(d1_ref, d2_ref, gate_ref, h1_ref, wf_ref, ys_ref, out_ref,
                    buf1, buf2, sem, *, tc):
    copies = []
    for u in range(tc):
        for d_ref, buf in ((d1_ref, buf1), (d2_ref, buf2)):
            cp = pltpu.make_async_copy(ys_ref.at[pl.ds(d_ref[0, 0, u], 1)],
                                       buf.at[pl.ds(u, 1)], sem)
            cp.start()
            copies.append(cp)
    for cp in copies:
        cp.wait()
    gates = gate_ref[...]
    h = h1_ref[...] + buf1[...] * gates[:, 0:1] + buf2[...] * gates[:, 1:2]
    out_ref[...] = _rms_scale(h) * wf_ref[...]


def _combine_call(d1, d2, gates, h1, wf, ys, tc):
    n_tok, d = h1.shape
    smem = pl.BlockSpec((1, 1, tc), lambda i: (i, 0, 0), memory_space=pltpu.SMEM)
    return pl.pallas_call(
        functools.partial(_combine_kernel, tc=tc),
        grid=(n_tok // tc,),
        in_specs=[
            smem, smem,
            pl.BlockSpec((tc, LANES), lambda i: (i, 0)),
            pl.BlockSpec((tc, d), lambda i: (i, 0)),
            _resident((1, d)),
            pl.BlockSpec(memory_space=pl.ANY),
        ],
        out_specs=pl.BlockSpec((tc, d), lambda i: (i, 0)),
        out_shape=jax.ShapeDtypeStruct((n_tok, d), F32),
        scratch_shapes=[
            pltpu.VMEM((tc, d), F32),
            pltpu.VMEM((tc, d), F32),
            pltpu.SemaphoreType.DMA(()),
        ],
        compiler_params=_params("arbitrary"),
        name="moe_combine",
    )(d1.reshape(n_tok // tc, 1, tc), d2.reshape(n_tok // tc, 1, tc), gates, h1, wf, ys)


def _tiles(seq, n_tok):
    return dict(
        proj=min(512, seq),
        gdn=min(512, seq),
        outp=min(512, n_tok),
        route=min(512, n_tok),
        disp=min(128, n_tok),
        bm=256,
        comb=min(128, n_tok),
    )


def _layer(h, n1w, w_in, conv_qkv_w, a_log, dt_bias, gdn_norm_w, conv_sc_w, w_out,
           n2w, wg_r, bg_r, we_r, be_r, w_gate, w_up, w_down, final_w, last):
    bsz, seq, d = h.shape
    n_tok = bsz * seq
    ts = _tiles(seq, n_tok)

    qkvz_cols = 4 * GDN_WIDTH
    w_qkvz = w_in[:, :qkvz_cols].astype(BF16)
    w_ba = jnp.pad(w_in[:, qkvz_cols:qkvz_cols + 2 * GDN_HEADS],
                   ((0, 0), (0, LANES - 2 * GDN_HEADS))).astype(BF16)
    sc0 = qkvz_cols + 2 * GDN_HEADS
    w_sc = w_in[:, sc0:].reshape(d, 3, SC_WIDTH // SC_TILE, SC_TILE)
    w_sc = jnp.swapaxes(w_sc, 1, 2).reshape(d, 3 * SC_WIDTH).astype(BF16)
    pad8 = (GDN_HEADS, LANES - 2 * GDN_HEADS)
    alog_v = jnp.pad(a_log, pad8).reshape(1, LANES)
    dtb_v = jnp.pad(dt_bias, pad8).reshape(1, LANES)

    qkvz, bg = _qkvz_call(h, n1w.reshape(1, d), w_qkvz, w_ba, conv_qkv_w, alog_v, dtb_v,
                          ts["proj"])
    y_sc = _sc_call(h, n1w.reshape(1, d), w_sc, conv_sc_w, ts["proj"])

    gt = jnp.swapaxes(bg[:, :, :2 * GDN_HEADS].reshape(bsz, seq // CHUNK, CHUNK, 2 * GDN_HEADS),
                      2, 3)
    y_gdn = _gdn_call(qkvz, bg, gt, gdn_norm_w.reshape(1, HEAD_DIM), ts["gdn"])

    n_router = N_GROUPS + N_EXPERTS
    wr_t = jnp.pad(jnp.concatenate([wg_r, we_r], axis=1).T, ((0, LANES - n_router), (0, 0)))
    rb = jnp.pad(jnp.concatenate([bg_r, be_r]), (0, LANES - n_router)).reshape(LANES, 1)
    x2d = h.reshape(n_tok, d)
    h1, hp, lt = _outproj_call(y_gdn.reshape(n_tok, GDN_WIDTH), y_sc.reshape(n_tok, SC_WIDTH),
                               x2d, w_out.astype(BF16), n2w.reshape(1, d), wr_t, rb, ts["outp"])

    idx, gates_t, cnt = _route_call(lt, ts["route"])
    bm = ts["bm"]
    counts = cnt[:, 0].astype(I32)
    padded = (counts + bm - 1) // bm * bm
    pad_end = jnp.cumsum(padded)
    pad_start = pad_end - padded
    d1 = pad_start[idx[0]] + idx[2]
    d2 = pad_start[idx[1]] + idx[3]
    nb = n_tok * TOP_K // bm + N_EXPERTS
    block_row = jnp.arange(nb, dtype=I32) * bm
    block_expert = jnp.minimum(
        jnp.sum((pad_end[None, :] <= block_row[:, None]).astype(I32), axis=1), N_EXPERTS - 1)
    n_used = (pad_end[-1:] // bm).astype(I32)
    gates = jnp.pad(gates_t[:TOP_K].T, ((0, 0), (0, LANES - TOP_K)))

    xs = _dispatch_call(d1, d2, hp, nb * bm, ts["disp"])
    ys = _expert_call(block_expert, n_used, xs, w_gate.astype(BF16), w_up.astype(BF16),
                      w_down.astype(BF16), bm)
    out = _combine_call(d1, d2, gates, h1, final_w.reshape(1, d), ys, ts["comb"])
    del last
    return out.reshape(bsz, seq, d)


def kernel(x, norm1_w, w_in, conv_qkv_w, a_log, dt_bias, gdn_norm_w, conv_sc_w, w_out, norm2_w,
           router_group_w, router_group_b, router_expert_w, router_expert_b, w_gate, w_up,
           w_down, final_norm_w):
    depth = norm1_w.shape[0]
    assert depth == 1, "the combine call applies the final RMSNorm, so a single layer is assumed"
    return _layer(x, norm1_w[0], w_in[0], conv_qkv_w[0], a_log[0], dt_bias[0], gdn_norm_w[0],
                  conv_sc_w[0], w_out[0], norm2_w[0], router_group_w[0], router_group_b[0],
                  router_expert_w[0], router_expert_b[0], w_gate[0], w_up[0], w_down[0],
                  final_norm_w, True)
```

```python
import functools

import jax
import jax.numpy as jnp
from jax import lax
from jax.experimental import pallas as pl
from jax.experimental.pallas import tpu as pltpu

F32 = jnp.float32
BF16 = jnp.bfloat16
I32 = jnp.int32
U32 = jnp.uint32
HIGHEST = lax.Precision.HIGHEST

EPS = 1e-6
D_MODEL = 2048
HEAD_DIM = 128
GDN_HEADS = 8
GDN_WIDTH = GDN_HEADS * HEAD_DIM
GDN_CONV = 4
CHUNK = 64
SC_WIDTH = 1024
SC_CONV = 3
N_GROUPS = 8
EXPERTS_PER_GROUP = 8
N_EXPERTS = N_GROUPS * EXPERTS_PER_GROUP
TOP_K = 2
D_EXPERT = 512
LANES = 128
SUBLANES = 8
SC_TILE = 256
VMEM_LIMIT = 56 * 1024 * 1024


def _sigmoid(x):
    return 1.0 / (1.0 + jnp.exp(-x))


def _silu(x):
    return x * _sigmoid(x)


def _softplus(x):
    return jnp.maximum(x, 0.0) + jnp.log1p(jnp.exp(-jnp.abs(x)))


def _rms_scale(x):
    return x * lax.rsqrt(jnp.mean(x * x, axis=-1, keepdims=True) + EPS)


def _dot(a, b):
    return jnp.dot(a, b, preferred_element_type=F32)


def _hdot(a, b):
    return jnp.dot(a, b, preferred_element_type=F32, precision=HIGHEST)


def _params(*sem):
    return pltpu.CompilerParams(dimension_semantics=sem, vmem_limit_bytes=VMEM_LIMIT)


def _causal_conv(p, carry_ref, cw, tm):
    taps = cw.shape[0]
    ext = jnp.concatenate([carry_ref[...], p], axis=0)
    carry_ref[...] = p[tm - SUBLANES:tm, :]
    y = cw[taps - 1:taps, :] * p
    for s in range(1, taps):
        shifted = pltpu.roll(ext, s, axis=0)[SUBLANES:SUBLANES + tm, :]
        y = y + cw[taps - 1 - s:taps - s, :] * shifted
    return y


def _resident(shape):
    return pl.BlockSpec(shape, lambda *_: (0,) * len(shape), pipeline_mode=pl.Buffered(1))


def _qkvz_kernel(x_ref, n1w_ref, w_ref, wba_ref, cw_ref, alog_ref, dtb_ref, ltri_ref,
                 out_ref, bg_ref, carry_sc, *, tm):
    @pl.when(pl.program_id(1) == 0)
    def _():
        carry_sc[...] = jnp.zeros_like(carry_sc)

    hn = (_rms_scale(x_ref[...]) * n1w_ref[...]).astype(BF16)

    ba = _dot(hn, wba_ref[...])
    lane = lax.broadcasted_iota(I32, ba.shape, 1)
    g = -jnp.exp(alog_ref[...]) * _softplus(ba + dtb_ref[...])
    g = jnp.where((lane >= GDN_HEADS) & (lane < 2 * GDN_HEADS), g, 0.0)
    gc = _hdot(ltri_ref[...], g)
    bg_ref[...] = jnp.where(lane < GDN_HEADS, _sigmoid(ba), gc)

    def project(j):
        return _dot(hn, w_ref[:, j * GDN_WIDTH:(j + 1) * GDN_WIDTH])

    def epilogue(j, p):
        cols = slice(j * GDN_WIDTH, (j + 1) * GDN_WIDTH)
        if j == 3:
            out_ref[:, cols] = p.astype(BF16)
            return
        y = _silu(_causal_conv(p, carry_sc.at[j], cw_ref[:, cols], tm))
        if j == 2:
            out_ref[:, cols] = y.astype(BF16)
            return
        scale = HEAD_DIM ** -0.5 if j == 0 else 1.0
        for h in range(GDN_HEADS):
            hc = slice(h * HEAD_DIM, (h + 1) * HEAD_DIM)
            yh = y[:, hc]
            yh = yh * (lax.rsqrt(jnp.sum(yh * yh, axis=-1, keepdims=True) + EPS) * scale)
            out_ref[:, j * GDN_WIDTH + h * HEAD_DIM:j * GDN_WIDTH + (h + 1) * HEAD_DIM] = (
                yh.astype(BF16))

    p_next = project(0)
    for j in range(4):
        p = p_next
        if j + 1 < 4:
            p_next = project(j + 1)
        epilogue(j, p)


def _qkvz_call(x, n1w, w, wba, cw, alog, dtb, tm):
    bsz, seq, d = x.shape
    n = w.shape[1]
    idx = jnp.arange(tm)
    ltri = ((idx[:, None] >= idx[None, :]) &
            (idx[:, None] // CHUNK == idx[None, :] // CHUNK)).astype(F32)
    return pl.pallas_call(
        functools.partial(_qkvz_kernel, tm=tm),
        grid=(bsz, seq // tm),
        in_specs=[
            pl.BlockSpec((None, tm, d), lambda b, i: (b, i, 0)),
            _resident((1, d)),
            _resident((d, n)),
            _resident((d, LANES)),
            _resident((GDN_CONV, 3 * GDN_WIDTH)),
            _resident((1, LANES)),
            _resident((1, LANES)),
            _resident((tm, tm)),
        ],
        out_specs=[
            pl.BlockSpec((None, tm, n), lambda b, i: (b, i, 0)),
            pl.BlockSpec((None, tm, LANES), lambda b, i: (b, i, 0)),
        ],
        out_shape=[
            jax.ShapeDtypeStruct((bsz, seq, n), BF16),
            jax.ShapeDtypeStruct((bsz, seq, LANES), F32),
        ],
        scratch_shapes=[pltpu.VMEM((3, SUBLANES, GDN_WIDTH), F32)],
        compiler_params=_params("arbitrary", "arbitrary"),
        name="qkvz_proj",
    )(x, n1w, w, wba, cw, alog, dtb, ltri)


def _sc_kernel(x_ref, n1w_ref, w_ref, cw_ref, out_ref, carry_sc, *, tm):
    @pl.when(pl.program_id(1) == 0)
    def _():
        carry_sc[...] = jnp.zeros_like(carry_sc)

    hn = (_rms_scale(x_ref[...]) * n1w_ref[...]).astype(BF16)
    n_tiles = SC_WIDTH // SC_TILE

    def project(t):
        return _dot(hn, w_ref[:, 3 * SC_TILE * t:3 * SC_TILE * (t + 1)])

    p_next = project(0)
    for t in range(n_tiles):
        p = p_next
        if t + 1 < n_tiles:
            p_next = project(t + 1)
        gate_b = p[:, 0:SC_TILE]
        ch = p[:, SC_TILE:2 * SC_TILE] * p[:, 2 * SC_TILE:3 * SC_TILE]
        y = _causal_conv(ch, carry_sc.at[t], cw_ref[:, SC_TILE * t:SC_TILE * (t + 1)], tm)
        out_ref[:, SC_TILE * t:SC_TILE * (t + 1)] = (gate_b * y).astype(BF16)


def _sc_call(x, n1w, w, cw, tm):
    bsz, seq, d = x.shape
    return pl.pallas_call(
        functools.partial(_sc_kernel, tm=tm),
        grid=(bsz, seq // tm),
        in_specs=[
            pl.BlockSpec((None, tm, d), lambda b, i: (b, i, 0)),
            _resident((1, d)),
            _resident((d, 3 * SC_WIDTH)),
            _resident((SC_CONV, SC_WIDTH)),
        ],
        out_specs=pl.BlockSpec((None, tm, SC_WIDTH), lambda b, i: (b, i, 0)),
        out_shape=jax.ShapeDtypeStruct((bsz, seq, SC_WIDTH), BF16),
        scratch_shapes=[pltpu.VMEM((SC_WIDTH // SC_TILE, SUBLANES, SC_TILE), F32)],
        compiler_params=_params("arbitrary", "arbitrary"),
        name="shortconv_proj",
    )(x, n1w, w, cw)


def _sdot(a, b):
    return _dot(a.astype(BF16), b.astype(BF16))


def _gdn_kernel(q_ref, k_ref, v_ref, z_ref, bg_ref, gt_ref, nw_ref, out_ref, state_sc, *, tc):
    @pl.when(pl.program_id(1) == 0)
    def _():
        state_sc[...] = jnp.zeros_like(state_sc)

    row = lax.broadcasted_iota(I32, (CHUNK, CHUNK), 0)
    col = lax.broadcasted_iota(I32, (CHUNK, CHUNK), 1)
    causal = row >= col
    strict = row > col
    eye = (row == col).astype(F32)
    same16 = (row // 16) == (col // 16)
    same32 = (row // 32) == (col // 32)
    m16 = same16.astype(F32)
    m32 = (same32 & jnp.logical_not(same16)).astype(F32)
    m64 = jnp.logical_not(same32).astype(F32)
    nw = nw_ref[...]
    tt = (((0,), (0,)), ((), ()))
    nt = (((1,), (1,)), ((), ()))
    heads = range(GDN_HEADS)

    def hcols(h):
        return slice(h * HEAD_DIM, (h + 1) * HEAD_DIM)

    def chunk_body(c, carry):
        r0 = pl.multiple_of(c * CHUNK, CHUNK)
        rows = pl.ds(r0, CHUNK)
        bgc = bg_ref[rows, :]
        gt = gt_ref[c]
        q = [q_ref[rows, hcols(h)] for h in heads]
        k = [k_ref[rows, hcols(h)] for h in heads]
        beta = [bgc[:, h:h + 1] for h in heads]
        gc = [bgc[:, GDN_HEADS + h:GDN_HEADS + h + 1] for h in heads]
        glast = [g[CHUNK - 1:CHUNK, :] for g in gc]

        kq = [lax.dot_general(jnp.concatenate([k[h], q[h]], axis=0), k[h], nt,
                              preferred_element_type=F32) for h in heads]
        a, attn = [], []
        for h in heads:
            gcr = gt[GDN_HEADS + h:GDN_HEADS + h + 1, :]
            decay = jnp.where(causal, jnp.exp(gc[h] - gcr), 0.0)
            a.append(jnp.where(strict, kq[h][:CHUNK] * beta[h] * decay, 0.0))
            attn.append((kq[h][CHUNK:] * decay).astype(BF16))

        n = [-(a[h] * m16) for h in heads]
        p = [eye + n[h] for h in heads]
        for _ in range(3):
            n = [_sdot(n[h], n[h]) for h in heads]
            p = [p[h] + _sdot(p[h], n[h]) for h in heads]
        for m in (m32, m64):
            e = [_sdot(a[h] * m, p[h]) for h in heads]
            p = [p[h] - _sdot(p[h], e[h]) for h in heads]

        eg = [jnp.exp(g) for g in gc]
        kf = [k[h].astype(F32) for h in heads]
        uw = []
        for h in heads:
            v = v_ref[rows, hcols(h)].astype(F32)
            rhs = jnp.concatenate([v * beta[h], kf[h] * (beta[h] * eg[h])], axis=1)
            uw.append(_sdot(p[h], rhs))

        state = [state_sc[h] for h in heads]
        rs = []
        for h in heads:
            wq = jnp.concatenate([uw[h][:, HEAD_DIM:], q[h].astype(F32) * eg[h]], axis=0)
            rs.append(_dot(wq.astype(BF16), state[h].astype(BF16)))
        vb = [(uw[h][:, :HEAD_DIM] - rs[h][:CHUNK]).astype(BF16) for h in heads]
        o = [rs[h][CHUNK:] + _dot(attn[h], vb[h]) for h in heads]
        for h in heads:
            k_dec = (kf[h] * jnp.exp(glast[h] - gc[h])).astype(BF16)
            state_sc[h] = state[h] * jnp.exp(glast[h]) + lax.dot_general(
                k_dec, vb[h], tt, preferred_element_type=F32)
        for h in heads:
            z = z_ref[rows, hcols(h)].astype(F32)
            out_ref[rows, hcols(h)] = (_rms_scale(o[h]) * nw * _silu(z)).astype(BF16)
        return carry

    lax.fori_loop(0, tc // CHUNK, chunk_body, 0)


def _gdn_call(qkvz, bg, gt, nw, tc):
    bsz, seq, _ = qkvz.shape

    def col(j):
        return pl.BlockSpec((None, tc, GDN_WIDTH), lambda b, i: (b, i, j))

    return pl.pallas_call(
        functools.partial(_gdn_kernel, tc=tc),
        grid=(bsz, seq // tc),
        in_specs=[
            col(0), col(1), col(2), col(3),
            pl.BlockSpec((None, tc, LANES), lambda b, i: (b, i, 0)),
            pl.BlockSpec((None, tc // CHUNK, 2 * GDN_HEADS, CHUNK), lambda b, i: (b, i, 0, 0)),
            _resident((1, HEAD_DIM)),
        ],
        out_specs=pl.BlockSpec((None, tc, GDN_WIDTH), lambda b, i: (b, i, 0)),
        out_shape=jax.ShapeDtypeStruct((bsz, seq, GDN_WIDTH), BF16),
        scratch_shapes=[pltpu.VMEM((GDN_HEADS, HEAD_DIM, HEAD_DIM), F32)],
        compiler_params=_params("arbitrary", "arbitrary"),
        name="gated_delta_rule",
    )(qkvz, qkvz, qkvz, qkvz, bg, gt, nw)


def _pack_bf16_pair(hi, lo):
    hb = lax.bitcast_convert_type(hi.astype(BF16).astype(F32), U32)
    lb = lax.bitcast_convert_type(lo.astype(BF16).astype(F32), U32)
    return (hb & jnp.uint32(0xFFFF0000)) | lax.shift_right_logical(lb, jnp.uint32(16))


def _unpack_bf16_pair(p):
    hi = lax.bitcast_convert_type(p & jnp.uint32(0xFFFF0000), F32).astype(BF16)
    lo = lax.bitcast_convert_type(lax.shift_left(p, jnp.uint32(16)), F32).astype(BF16)
    return hi, lo


def _outproj_kernel(yg_ref, ys_ref, x_ref, wo_ref, n2w_ref, wr_ref, rb_ref,
                    h1_ref, hp_ref, lt_ref):
    h1 = (x_ref[...] + _dot(yg_ref[...], wo_ref[0:GDN_WIDTH, :])
          + _dot(ys_ref[...], wo_ref[GDN_WIDTH:GDN_WIDTH + SC_WIDTH, :]))
    h1_ref[...] = h1
    hn = _rms_scale(h1) * n2w_ref[...]
    hn_hi = hn.astype(BF16)
    hn_lo = (hn - hn_hi.astype(F32)).astype(BF16)
    lg = _dot(hn_hi, wr_ref[...])
    lg = lg[:, :LANES] + lg[:, LANES:] + _dot(hn_lo, wr_ref[:, :LANES])
    lt_ref[...] = lg.T + rb_ref[...]
    half = D_MODEL // 2
    hp_ref[...] = _pack_bf16_pair(hn[:, :half], hn[:, half:])


def _outproj_call(yg, ys, x2d, wo, n2w, wr_t, rb, tm):
    n_tok, d = x2d.shape
    return pl.pallas_call(
        _outproj_kernel,
        grid=(n_tok // tm,),
        in_specs=[
            pl.BlockSpec((tm, GDN_WIDTH), lambda i: (i, 0)),
            pl.BlockSpec((tm, SC_WIDTH), lambda i: (i, 0)),
            pl.BlockSpec((tm, d), lambda i: (i, 0)),
            _resident((GDN_WIDTH + SC_WIDTH, d)),
            _resident((1, d)),
            _resident((d, 2 * LANES)),
            _resident((LANES, 1)),
        ],
        out_specs=[
            pl.BlockSpec((tm, d), lambda i: (i, 0)),
            pl.BlockSpec((tm, d // 2), lambda i: (i, 0)),
            pl.BlockSpec((LANES, tm), lambda i: (0, i)),
        ],
        out_shape=[
            jax.ShapeDtypeStruct((n_tok, d), F32),
            jax.ShapeDtypeStruct((n_tok, d // 2), U32),
            jax.ShapeDtypeStruct((LANES, n_tok), F32),
        ],
        compiler_params=_params("arbitrary"),
        name="out_proj_router",
    )(yg, ys, x2d, wo, n2w, wr_t, rb)


def _first_argmax(vals, iota, size):
    m = jnp.max(vals, axis=0, keepdims=True)
    idx = jnp.min(jnp.where(vals == m, iota, size), axis=0, keepdims=True)
    return m, idx


def _route_kernel(lt_ref, idx_ref, gate_ref, cnt_ref, carry_sc, *, tr):
    @pl.when(pl.program_id(0) == 0)
    def _():
        carry_sc[...] = jnp.zeros_like(carry_sc)

    iota8 = lax.broadcasted_iota(I32, (EXPERTS_PER_GROUP, tr), 0)
    gl = lt_ref[0:N_GROUPS, :]
    gmax, gidx = _first_argmax(gl, iota8, N_GROUPS)
    g_w = 1.0 / jnp.sum(jnp.exp(gl - gmax), axis=0, keepdims=True)

    ig = jnp.zeros((EXPERTS_PER_GROUP, tr), F32)
    for g in range(N_GROUPS):
        lo = N_GROUPS + g * EXPERTS_PER_GROUP
        ig = jnp.where(gidx == g, lt_ref[lo:lo + EXPERTS_PER_GROUP, :], ig)
    pe = jnp.exp(ig - jnp.max(ig, axis=0, keepdims=True))
    p = pe / jnp.sum(pe, axis=0, keepdims=True)
    p1, i1 = _first_argmax(p, iota8, EXPERTS_PER_GROUP)
    p2, i2 = _first_argmax(jnp.where(iota8 == i1, -1.0, p), iota8, EXPERTS_PER_GROUP)
    denom = p1 + p2
    e1 = gidx * EXPERTS_PER_GROUP + i1
    e2 = gidx * EXPERTS_PER_GROUP + i2

    iota_e = lax.broadcasted_iota(I32, (N_EXPERTS, tr), 0)
    oh1 = (iota_e == e1).astype(F32)
    oh2 = (iota_e == e2).astype(F32)
    ohs = oh1 + oh2
    r = lax.broadcasted_iota(I32, (tr, tr), 0)
    c = lax.broadcasted_iota(I32, (tr, tr), 1)
    before = jnp.where(r < c, 1.0, 0.0).astype(BF16)
    base = carry_sc[:, 0:1] + _dot(ohs.astype(BF16), before)
    rank1 = jnp.sum(oh1 * base, axis=0, keepdims=True)
    rank2 = jnp.sum(oh2 * base, axis=0, keepdims=True)
    carry_sc[...] = carry_sc[...] + jnp.sum(ohs, axis=1, keepdims=True)
    cnt_ref[...] = carry_sc[...]

    idx_ref[...] = jnp.zeros_like(idx_ref)
    idx_ref[0:1, :] = e1
    idx_ref[1:2, :] = e2
    idx_ref[2:3, :] = rank1.astype(I32)
    idx_ref[3:4, :] = rank2.astype(I32)
    gate_ref[...] = jnp.zeros_like(gate_ref)
    gate_ref[0:1, :] = g_w * p1 / denom
    gate_ref[1:2, :] = g_w * p2 / denom


def _route_call(lt, tr):
    n_tok = lt.shape[1]
    return pl.pallas_call(
        functools.partial(_route_kernel, tr=tr),
        grid=(n_tok // tr,),
        in_specs=[pl.BlockSpec((LANES, tr), lambda i: (0, i))],
        out_specs=[
            pl.BlockSpec((SUBLANES, tr), lambda i: (0, i)),
            pl.BlockSpec((SUBLANES, tr), lambda i: (0, i)),
            pl.BlockSpec((N_EXPERTS, LANES), lambda i: (0, 0)),
        ],
        out_shape=[
            jax.ShapeDtypeStruct((SUBLANES, n_tok), I32),
            jax.ShapeDtypeStruct((SUBLANES, n_tok), F32),
            jax.ShapeDtypeStruct((N_EXPERTS, LANES), F32),
        ],
        scratch_shapes=[pltpu.VMEM((N_EXPERTS, LANES), F32)],
        compiler_params=_params("arbitrary"),
        name="moe_route",
    )(lt)


DISPATCH_SLOTS = 3


def _dispatch_kernel(d1_ref, d2_ref, d1p_ref, d2p_ref, hp_ref, xs_in_ref, xs_ref,
                     stage, sem_in, sem_out, *, tb):
    del xs_in_ref
    i = pl.program_id(0)
    last = pl.num_programs(0) - 1

    def block_load(step):
        s = step % DISPATCH_SLOTS
        return pltpu.make_async_copy(hp_ref.at[pl.ds(step * tb, tb)], stage.at[s], sem_in.at[s])

    def row_copies(step, da_ref, db_ref):
        s = step % DISPATCH_SLOTS
        return [pltpu.make_async_copy(stage.at[s, pl.ds(u, 1)],
                                      xs_ref.at[pl.ds(d_ref[0, 0, u], 1)], sem_out.at[s])
                for u in range(tb) for d_ref in (da_ref, db_ref)]

    @pl.when(i == 0)
    def _():
        block_load(i).start()

    @pl.when(i < last)
    def _():
        block_load(i + 1).start()

    block_load(i).wait()
    for cp in row_copies(i, d1_ref, d2_ref):
        cp.start()

    @pl.when(i > 0)
    def _():
        for cp in row_copies(i - 1, d1p_ref, d2p_ref):
            cp.wait()

    @pl.when(i == last)
    def _():
        for cp in row_copies(i, d1_ref, d2_ref):
            cp.wait()


def _dispatch_call(d1, d2, hp, n_rows, tb):
    n_tok, width = hp.shape
    xs0 = jnp.zeros((n_rows, width), U32)
    cur = pl.BlockSpec((1, 1, tb), lambda i: (i, 0, 0), memory_space=pltpu.SMEM)
    prev = pl.BlockSpec((1, 1, tb), lambda i: (jnp.maximum(i - 1, 0), 0, 0),
                        memory_space=pltpu.SMEM)
    d1 = d1.reshape(n_tok // tb, 1, tb)
    d2 = d2.reshape(n_tok // tb, 1, tb)
    return pl.pallas_call(
        functools.partial(_dispatch_kernel, tb=tb),
        grid=(n_tok // tb,),
        in_specs=[
            cur, cur, prev, prev,
            pl.BlockSpec(memory_space=pl.ANY),
            pl.BlockSpec(memory_space=pl.ANY),
        ],
        out_specs=pl.BlockSpec(memory_space=pl.ANY),
        out_shape=jax.ShapeDtypeStruct((n_rows, width), U32),
        scratch_shapes=[
            pltpu.VMEM((DISPATCH_SLOTS, tb, width), U32),
            pltpu.SemaphoreType.DMA((DISPATCH_SLOTS,)),
            pltpu.SemaphoreType.DMA((DISPATCH_SLOTS,)),
        ],
        input_output_aliases={5: 0},
        compiler_params=_params("arbitrary"),
        name="moe_dispatch",
    )(d1, d2, d1, d2, hp, xs0)


def _expert_kernel(be_ref, nu_ref, xs_ref, wg_ref, wu_ref, wd_ref, ys_ref):
    del be_ref
    used = pl.program_id(0) < nu_ref[0]

    @pl.when(used)
    def _():
        hi, lo = _unpack_bf16_pair(xs_ref[...])
        xb = jnp.concatenate([hi, lo], axis=1)
        hid = _silu(_dot(xb, wg_ref[...])) * _dot(xb, wu_ref[...])
        ys_ref[...] = _dot(hid.astype(BF16), wd_ref[...])

    @pl.when(jnp.logical_not(used))
    def _():
        ys_ref[...] = jnp.zeros_like(ys_ref)


def _expert_call(block_expert, n_used, xs, wg, wu, wd, bm):
    n_rows, width = xs.shape
    d = 2 * width
    nb = n_rows // bm
    grid_spec = pltpu.PrefetchScalarGridSpec(
        num_scalar_prefetch=2,
        grid=(nb,),
        in_specs=[
            pl.BlockSpec((bm, width), lambda i, be, nu: (jnp.minimum(i, nu[0] - 1), 0)),
            pl.BlockSpec((None, d, D_EXPERT), lambda i, be, nu: (be[i], 0, 0)),
            pl.BlockSpec((None, d, D_EXPERT), lambda i, be, nu: (be[i], 0, 0)),
            pl.BlockSpec((None, D_EXPERT, d), lambda i, be, nu: (be[i], 0, 0)),
        ],
        out_specs=pl.BlockSpec((bm, d), lambda i, be, nu: (i, 0)),
    )
    return pl.pallas_call(
        _expert_kernel,
        grid_spec=grid_spec,
        out_shape=jax.ShapeDtypeStruct((n_rows, d), F32),
        compiler_params=_params("arbitrary"),
        name="moe_experts",
    )(block_expert, n_used, xs, wg, wu, wd)


def _combine_kernel(d1_ref, d2_ref, d1n_ref, d2n_ref, gate_ref, h1_ref, wf_ref, ys_ref, out_ref,
                    buf, sem, *, tc):
    i = pl.program_id(0)
    slot = i % 2

    def row_copies(da_ref, db_ref, s):
        return [pltpu.make_async_copy(ys_ref.at[pl.ds(d_ref[0, 0, u], 1)],
                                      buf.at[s, k, pl.ds(u, 1)], sem.at[s])
                for u in range(tc) for k, d_ref in enumerate((da_ref, db_ref))]

    @pl.when(i == 0)
    def _():
        for cp in row_copies(d1_ref, d2_ref, 0):
            cp.start()

    @pl.when(i + 1 < pl.num_programs(0))
    def _():
        for cp in row_copies(d1n_ref, d2n_ref, 1 - slot):
            cp.start()

    for cp in row_copies(d1_ref, d2_ref, slot):
        cp.wait()
    gates = gate_ref[...]
    h = h1_ref[...] + buf[slot, 0] * gates[:, 0:1] + buf[slot, 1] * gates[:, 1:2]
    out_ref[...] = _rms_scale(h) * wf_ref[...]


def _combine_call(d1, d2, gates, h1, wf, ys, tc):
    n_tok, d = h1.shape
    steps = n_tok // tc
    cur = pl.BlockSpec((1, 1, tc), lambda i: (i, 0, 0), memory_space=pltpu.SMEM)
    nxt = pl.BlockSpec((1, 1, tc), lambda i: (jnp.minimum(i + 1, steps - 1), 0, 0),
                       memory_space=pltpu.SMEM)
    d1 = d1.reshape(steps, 1, tc)
    d2 = d2.reshape(steps, 1, tc)
    return pl.pallas_call(
        functools.partial(_combine_kernel, tc=tc),
        grid=(steps,),
        in_specs=[
            cur, cur, nxt, nxt,
            pl.BlockSpec((tc, LANES), lambda i: (i, 0)),
            pl.BlockSpec((tc, d), lambda i: (i, 0)),
            _resident((1, d)),
            pl.BlockSpec(memory_space=pl.ANY),
        ],
        out_specs=pl.BlockSpec((tc, d), lambda i: (i, 0)),
        out_shape=jax.ShapeDtypeStruct((n_tok, d), F32),
        scratch_shapes=[
            pltpu.VMEM((2, TOP_K, tc, d), F32),
            pltpu.SemaphoreType.DMA((2,)),
        ],
        compiler_params=_params("arbitrary"),
        name="moe_combine",
    )(d1, d2, d1, d2, gates, h1, wf, ys)


def _tiles(seq, n_tok):
    return dict(
        proj=min(512, seq),
        gdn=min(512, seq),
        outp=min(512, n_tok),
        route=min(512, n_tok),
        disp=min(256, n_tok),
        bm=256,
        comb=min(256, n_tok),
    )


def _layer(h, n1w, w_in, conv_qkv_w, a_log, dt_bias, gdn_norm_w, conv_sc_w, w_out,
           n2w, wg_r, bg_r, we_r, be_r, w_gate, w_up, w_down, final_w, last):
    bsz, seq, d = h.shape
    n_tok = bsz * seq
    ts = _tiles(seq, n_tok)

    qkvz_cols = 4 * GDN_WIDTH
    w_qkvz = w_in[:, :qkvz_cols].astype(BF16)
    w_ba = jnp.pad(w_in[:, qkvz_cols:qkvz_cols + 2 * GDN_HEADS],
                   ((0, 0), (0, LANES - 2 * GDN_HEADS))).astype(BF16)
    sc0 = qkvz_cols + 2 * GDN_HEADS
    w_sc = w_in[:, sc0:].reshape(d, 3, SC_WIDTH // SC_TILE, SC_TILE)
    w_sc = jnp.swapaxes(w_sc, 1, 2).reshape(d, 3 * SC_WIDTH).astype(BF16)
    pad8 = (GDN_HEADS, LANES - 2 * GDN_HEADS)
    alog_v = jnp.pad(a_log, pad8).reshape(1, LANES)
    dtb_v = jnp.pad(dt_bias, pad8).reshape(1, LANES)

    qkvz, bg = _qkvz_call(h, n1w.reshape(1, d), w_qkvz, w_ba, conv_qkv_w, alog_v, dtb_v,
                          ts["proj"])
    y_sc = _sc_call(h, n1w.reshape(1, d), w_sc, conv_sc_w, ts["proj"])

    gt = jnp.swapaxes(bg[:, :, :2 * GDN_HEADS].reshape(bsz, seq // CHUNK, CHUNK, 2 * GDN_HEADS),
                      2, 3)
    y_gdn = _gdn_call(qkvz, bg, gt, gdn_norm_w.reshape(1, HEAD_DIM), ts["gdn"])

    n_router = N_GROUPS + N_EXPERTS
    wr = jnp.pad(jnp.concatenate([wg_r, we_r], axis=1), ((0, 0), (0, LANES - n_router)))
    wr_hi = wr.astype(BF16)
    wr_t = jnp.concatenate([wr_hi, (wr - wr_hi.astype(F32)).astype(BF16)], axis=1)
    rb = jnp.pad(jnp.concatenate([bg_r, be_r]), (0, LANES - n_router)).reshape(LANES, 1)
    x2d = h.reshape(n_tok, d)
    h1, hp, lt = _outproj_call(y_gdn.reshape(n_tok, GDN_WIDTH), y_sc.reshape(n_tok, SC_WIDTH),
                               x2d, w_out.astype(BF16), n2w.reshape(1, d), wr_t, rb, ts["outp"])

    idx, gates_t, cnt = _route_call(lt, ts["route"])
    bm = ts["bm"]
    counts = cnt[:, 0].astype(I32)
    padded = (counts + bm - 1) // bm * bm
    pad_end = jnp.cumsum(padded)
    pad_start = pad_end - padded
    d1 = pad_start[idx[0]] + idx[2]
    d2 = pad_start[idx[1]] + idx[3]
    nb = n_tok * TOP_K // bm + N_EXPERTS
    block_row = jnp.arange(nb, dtype=I32) * bm
    block_expert = jnp.minimum(
        jnp.sum((pad_end[None, :] <= block_row[:, None]).astype(I32), axis=1), N_EXPERTS - 1)
    n_used = (pad_end[-1:] // bm).astype(I32)
    gates = jnp.pad(gates_t[:TOP_K].T, ((0, 0), (0, LANES - TOP_K)))

    xs = _dispatch_call(d1, d2, hp, nb * bm, ts["disp"])
    ys = _expert_call(block_expert, n_used, xs, w_gate.astype(BF16), w_up.astype(BF16),
                      w_down.astype(BF16), bm)
    out = _combine_call(d1, d2, gates, h1, final_w.reshape(1, d), ys, ts["comb"])
    del last
    return out.reshape(bsz, seq, d)


def kernel(x, norm1_w, w_in, conv_qkv_w, a_log, dt_bias, gdn_norm_w, conv_sc_w, w_out, norm2_w,
           router_group_w, router_group_b, router_expert_w, router_expert_b, w_gate, w_up,
           w_down, final_norm_w):
    depth = norm1_w.shape[0]
    assert depth == 1, "the combine call applies the final RMSNorm, so a single layer is assumed"
    return _layer(x, norm1_w[0], w_in[0], conv_qkv_w[0], a_log[0], dt_bias[0], gdn_norm_w[0],
                  conv_sc_w[0], w_out[0], norm2_w[0], router_group_w[0], router_group_b[0],
                  router_expert_w[0], router_expert_b[0], w_gate[0], w_up[0], w_down[0],
                  final_norm_w, True)
```

```python
import functools

import jax
import jax.numpy as jnp
from jax import lax
from jax.experimental import pallas as pl
from jax.experimental.pallas import tpu as pltpu

F32 = jnp.float32
BF16 = jnp.bfloat16
I32 = jnp.int32
U32 = jnp.uint32
HIGHEST = lax.Precision.HIGHEST

EPS = 1e-6
D_MODEL = 2048
HEAD_DIM = 128
GDN_HEADS = 8
GDN_WIDTH = GDN_HEADS * HEAD_DIM
GDN_CONV = 4
CHUNK = 64
SC_WIDTH = 1024
SC_CONV = 3
N_GROUPS = 8
EXPERTS_PER_GROUP = 8
N_EXPERTS = N_GROUPS * EXPERTS_PER_GROUP
TOP_K = 2
D_EXPERT = 512
LANES = 128
SUBLANES = 8
SC_TILE = 256
VMEM_LIMIT = 56 * 1024 * 1024


def _sigmoid(x):
    return 1.0 / (1.0 + jnp.exp(-x))


def _silu(x):
    return x * _sigmoid(x)


def _softplus(x):
    return jnp.maximum(x, 0.0) + jnp.log1p(jnp.exp(-jnp.abs(x)))


def _rms_scale(x):
    return x * lax.rsqrt(jnp.mean(x * x, axis=-1, keepdims=True) + EPS)


def _dot(a, b):
    return jnp.dot(a, b, preferred_element_type=F32)


def _hdot(a, b):
    return jnp.dot(a, b, preferred_element_type=F32, precision=HIGHEST)


def _params(*sem):
    return pltpu.CompilerParams(dimension_semantics=sem, vmem_limit_bytes=VMEM_LIMIT)


def _causal_conv(p, carry_ref, cw, tm):
    taps = cw.shape[0]
    ext = jnp.concatenate([carry_ref[...], p], axis=0)
    carry_ref[...] = p[tm - SUBLANES:tm, :]
    y = cw[taps - 1:taps, :] * p
    for s in range(1, taps):
        shifted = pltpu.roll(ext, s, axis=0)[SUBLANES:SUBLANES + tm, :]
        y = y + cw[taps - 1 - s:taps - s, :] * shifted
    return y


def _resident(shape):
    return pl.BlockSpec(shape, lambda *_: (0,) * len(shape), pipeline_mode=pl.Buffered(1))


SC_COL0 = 4 * GDN_WIDTH
PROJ_ORDER = (("qkvz", 0), ("sc", 0), ("qkvz", 1), ("sc", 1), ("qkvz", 2), ("sc", 2),
              ("qkvz", 3), ("sc", 3))


def _inproj_kernel(x_ref, n1w_ref, w_ref, wba_ref, cwq_ref, cws_ref, alog_ref, dtb_ref, ltri_ref,
                   qkvz_ref, ysc_ref, bg_ref, carry_q, carry_s, *, tm):
    @pl.when(pl.program_id(1) == 0)
    def _():
        carry_q[...] = jnp.zeros_like(carry_q)
        carry_s[...] = jnp.zeros_like(carry_s)

    hn = (_rms_scale(x_ref[...]) * n1w_ref[...]).astype(BF16)

    ba = _dot(hn, wba_ref[...])
    lane = lax.broadcasted_iota(I32, ba.shape, 1)
    g = -jnp.exp(alog_ref[...]) * _softplus(ba + dtb_ref[...])
    g = jnp.where((lane >= GDN_HEADS) & (lane < 2 * GDN_HEADS), g, 0.0)
    gc = _hdot(ltri_ref[...], g)
    bg_ref[...] = jnp.where(lane < GDN_HEADS, _sigmoid(ba), gc)

    def project(kind, j):
        if kind == "qkvz":
            return _dot(hn, w_ref[:, j * GDN_WIDTH:(j + 1) * GDN_WIDTH])
        lo = SC_COL0 + 3 * SC_TILE * j
        return _dot(hn, w_ref[:, lo:lo + 3 * SC_TILE])

    def epilogue(kind, j, p):
        if kind == "sc":
            cols = slice(SC_TILE * j, SC_TILE * (j + 1))
            ch = p[:, SC_TILE:2 * SC_TILE] * p[:, 2 * SC_TILE:3 * SC_TILE]
            y = _causal_conv(ch, carry_s.at[j], cws_ref[:, cols], tm)
            ysc_ref[:, cols] = (p[:, 0:SC_TILE] * y).astype(BF16)
            return
        cols = slice(j * GDN_WIDTH, (j + 1) * GDN_WIDTH)
        if j == 3:
            qkvz_ref[:, cols] = p.astype(BF16)
            return
        y = _silu(_causal_conv(p, carry_q.at[j], cwq_ref[:, cols], tm))
        if j == 2:
            qkvz_ref[:, cols] = y.astype(BF16)
            return
        scale = HEAD_DIM ** -0.5 if j == 0 else 1.0
        for h in range(GDN_HEADS):
            hc = slice(h * HEAD_DIM, (h + 1) * HEAD_DIM)
            yh = y[:, hc]
            yh = yh * (lax.rsqrt(jnp.sum(yh * yh, axis=-1, keepdims=True) + EPS) * scale)
            qkvz_ref[:, j * GDN_WIDTH + h * HEAD_DIM:j * GDN_WIDTH + (h + 1) * HEAD_DIM] = (
                yh.astype(BF16))

    p_next = project(*PROJ_ORDER[0])
    for n, tile in enumerate(PROJ_ORDER):
        p = p_next
        if n + 1 < len(PROJ_ORDER):
            p_next = project(*PROJ_ORDER[n + 1])
        epilogue(*tile, p)


def _inproj_call(x, n1w, w, wba, cwq, cws, alog, dtb, tm):
    bsz, seq, d = x.shape
    idx = jnp.arange(tm)
    ltri = ((idx[:, None] >= idx[None, :]) &
            (idx[:, None] // CHUNK == idx[None, :] // CHUNK)).astype(F32)

    def rows(width):
        return pl.BlockSpec((None, tm, width), lambda b, i: (b, i, 0))

    return pl.pallas_call(
        functools.partial(_inproj_kernel, tm=tm),
        grid=(bsz, seq // tm),
        in_specs=[
            rows(d),
            _resident((1, d)),
            _resident((d, SC_COL0 + 3 * SC_WIDTH)),
            _resident((d, LANES)),
            _resident((GDN_CONV, 3 * GDN_WIDTH)),
            _resident((SC_CONV, SC_WIDTH)),
            _resident((1, LANES)),
            _resident((1, LANES)),
            _resident((tm, tm)),
        ],
        out_specs=[rows(4 * GDN_WIDTH), rows(SC_WIDTH), rows(LANES)],
        out_shape=[
            jax.ShapeDtypeStruct((bsz, seq, 4 * GDN_WIDTH), BF16),
            jax.ShapeDtypeStruct((bsz, seq, SC_WIDTH), BF16),
            jax.ShapeDtypeStruct((bsz, seq, LANES), F32),
        ],
        scratch_shapes=[
            pltpu.VMEM((3, SUBLANES, GDN_WIDTH), F32),
            pltpu.VMEM((SC_WIDTH // SC_TILE, SUBLANES, SC_TILE), F32),
        ],
        compiler_params=_params("arbitrary", "arbitrary"),
        name="in_proj",
    )(x, n1w, w, wba, cwq, cws, alog, dtb, ltri)


GDN_CHUNKS_PER_TRIP = 2


def _sdot(a, b):
    return _dot(a.astype(BF16), b.astype(BF16))


def _gdn_kernel(q_ref, k_ref, v_ref, z_ref, bg_ref, gt_ref, nw_ref, out_ref, state_sc, *, tc):
    @pl.when(pl.program_id(1) == 0)
    def _():
        state_sc[...] = jnp.zeros_like(state_sc)

    row = lax.broadcasted_iota(I32, (CHUNK, CHUNK), 0)
    col = lax.broadcasted_iota(I32, (CHUNK, CHUNK), 1)
    causal = row >= col
    strict = row > col
    eye = (row == col).astype(F32)
    same16 = (row // 16) == (col // 16)
    same32 = (row // 32) == (col // 32)
    m16 = same16.astype(F32)
    m32 = (same32 & jnp.logical_not(same16)).astype(F32)
    m64 = jnp.logical_not(same32).astype(F32)
    nw = nw_ref[...]
    tt = (((0,), (0,)), ((), ()))
    nt = (((1,), (1,)), ((), ()))
    heads = range(GDN_HEADS)

    def hcols(h):
        return slice(h * HEAD_DIM, (h + 1) * HEAD_DIM)

    def trip_body(t, carry):
        units = []
        for cc in range(GDN_CHUNKS_PER_TRIP):
            c = t * GDN_CHUNKS_PER_TRIP + cc
            rows = pl.ds(pl.multiple_of(c * CHUNK, CHUNK), CHUNK)
            bgc = bg_ref[rows, :]
            gt = gt_ref[c]
            for h in heads:
                gc = bgc[:, GDN_HEADS + h:GDN_HEADS + h + 1]
                units.append(dict(
                    rows=rows, h=h, gc=gc,
                    beta=bgc[:, h:h + 1],
                    gcr=gt[GDN_HEADS + h:GDN_HEADS + h + 1, :],
                    glast=gc[CHUNK - 1:CHUNK, :],
                    q=q_ref[rows, hcols(h)], k=k_ref[rows, hcols(h)]))

        for u in units:
            u["kq"] = lax.dot_general(jnp.concatenate([u["k"], u["q"]], axis=0), u["k"], nt,
                                      preferred_element_type=F32)
        for u in units:
            decay = jnp.where(causal, jnp.exp(u["gc"] - u["gcr"]), 0.0)
            u["a"] = jnp.where(strict, u["kq"][:CHUNK] * u["beta"] * decay, 0.0)
            u["attn"] = (u["kq"][CHUNK:] * decay).astype(BF16)

        for u in units:
            u["n"] = -(u["a"] * m16)
            u["p"] = eye + u["n"]
        for _ in range(3):
            for u in units:
                u["n"] = _sdot(u["n"], u["n"])
            for u in units:
                u["p"] = u["p"] + _sdot(u["p"], u["n"])
        for m in (m32, m64):
            for u in units:
                u["e"] = _sdot(u["a"] * m, u["p"])
            for u in units:
                u["p"] = u["p"] - _sdot(u["p"], u["e"])

        for u in units:
            u["eg"] = jnp.exp(u["gc"])
            u["kf"] = u["k"].astype(F32)
            v = v_ref[u["rows"], hcols(u["h"])].astype(F32)
            bw = u["beta"] * u["eg"]
            rhs = jnp.concatenate([v * u["beta"], u["kf"] * bw], axis=1)
            u["uw"] = _sdot(u["p"], rhs)

        state = [state_sc[h] for h in heads]
        for cc in range(GDN_CHUNKS_PER_TRIP):
            us = units[cc * GDN_HEADS:(cc + 1) * GDN_HEADS]
            for u in us:
                wq = jnp.concatenate([u["uw"][:, HEAD_DIM:], u["q"].astype(F32) * u["eg"]], axis=0)
                u["rs"] = _dot(wq.astype(BF16), state[u["h"]].astype(BF16))
            for u in us:
                u["vb"] = (u["uw"][:, :HEAD_DIM] - u["rs"][:CHUNK]).astype(BF16)
                u["o"] = u["rs"][CHUNK:] + _dot(u["attn"], u["vb"])
            for u in us:
                k_dec = (u["kf"] * jnp.exp(u["glast"] - u["gc"])).astype(BF16)
                state[u["h"]] = state[u["h"]] * jnp.exp(u["glast"]) + lax.dot_general(
                    k_dec, u["vb"], tt, preferred_element_type=F32)
            for u in us:
                z = z_ref[u["rows"], hcols(u["h"])].astype(F32)
                out_ref[u["rows"], hcols(u["h"])] = (
                    _rms_scale(u["o"]) * nw * _silu(z)).astype(BF16)
        for h in heads:
            state_sc[h] = state[h]
        return carry

    lax.fori_loop(0, tc // (CHUNK * GDN_CHUNKS_PER_TRIP), trip_body, 0)


def _gdn_call(qkvz, bg, gt, nw, tc):
    bsz, seq, _ = qkvz.shape

    def col(j):
        return pl.BlockSpec((None, tc, GDN_WIDTH), lambda b, i: (b, i, j))

    return pl.pallas_call(
        functools.partial(_gdn_kernel, tc=tc),
        grid=(bsz, seq // tc),
        in_specs=[
            col(0), col(1), col(2), col(3),
            pl.BlockSpec((None, tc, LANES), lambda b, i: (b, i, 0)),
            pl.BlockSpec((None, tc // CHUNK, 2 * GDN_HEADS, CHUNK), lambda b, i: (b, i, 0, 0)),
            _resident((1, HEAD_DIM)),
        ],
        out_specs=pl.BlockSpec((None, tc, GDN_WIDTH), lambda b, i: (b, i, 0)),
        out_shape=jax.ShapeDtypeStruct((bsz, seq, GDN_WIDTH), BF16),
        scratch_shapes=[pltpu.VMEM((GDN_HEADS, HEAD_DIM, HEAD_DIM), F32)],
        compiler_params=_params("arbitrary", "arbitrary"),
        name="gated_delta_rule",
    )(qkvz, qkvz, qkvz, qkvz, bg, gt, nw)


def _pack_bf16_pair(hi, lo):
    hb = lax.bitcast_convert_type(hi.astype(BF16).astype(F32), U32)
    lb = lax.bitcast_convert_type(lo.astype(BF16).astype(F32), U32)
    return (hb & jnp.uint32(0xFFFF0000)) | lax.shift_right_logical(lb, jnp.uint32(16))


def _unpack_bf16_pair(p):
    hi = lax.bitcast_convert_type(p & jnp.uint32(0xFFFF0000), F32).astype(BF16)
    lo = lax.bitcast_convert_type(lax.shift_left(p, jnp.uint32(16)), F32).astype(BF16)
    return hi, lo


def _outproj_kernel(yg_ref, ys_ref, x_ref, wo_ref, n2w_ref, wr_ref, rb_ref,
                    h1_ref, hp_ref, lt_ref):
    h1 = (x_ref[...] + _dot(yg_ref[...], wo_ref[0:GDN_WIDTH, :])
          + _dot(ys_ref[...], wo_ref[GDN_WIDTH:GDN_WIDTH + SC_WIDTH, :]))
    h1_ref[...] = h1
    hn = _rms_scale(h1) * n2w_ref[...]
    hn_hi = hn.astype(BF16)
    hn_lo = (hn - hn_hi.astype(F32)).astype(BF16)
    lg = _dot(hn_hi, wr_ref[...])
    lg = lg[:, :LANES] + lg[:, LANES:] + _dot(hn_lo, wr_ref[:, :LANES])
    lt_ref[...] = lg.T + rb_ref[...]
    half = D_MODEL // 2
    hp_ref[...] = _pack_bf16_pair(hn[:, :half], hn[:, half:])


def _outproj_call(yg, ys, x2d, wo, n2w, wr_t, rb, tm):
    n_tok, d = x2d.shape
    return pl.pallas_call(
        _outproj_kernel,
        grid=(n_tok // tm,),
        in_specs=[
            pl.BlockSpec((tm, GDN_WIDTH), lambda i: (i, 0)),
            pl.BlockSpec((tm, SC_WIDTH), lambda i: (i, 0)),
            pl.BlockSpec((tm, d), lambda i: (i, 0)),
            _resident((GDN_WIDTH + SC_WIDTH, d)),
            _resident((1, d)),
            _resident((d, 2 * LANES)),
            _resident((LANES, 1)),
        ],
        out_specs=[
            pl.BlockSpec((tm, d), lambda i: (i, 0)),
            pl.BlockSpec((tm, d // 2), lambda i: (i, 0)),
            pl.BlockSpec((LANES, tm), lambda i: (0, i)),
        ],
        out_shape=[
            jax.ShapeDtypeStruct((n_tok, d), F32),
            jax.ShapeDtypeStruct((n_tok, d // 2), U32),
            jax.ShapeDtypeStruct((LANES, n_tok), F32),
        ],
        compiler_params=_params("arbitrary"),
        name="out_proj_router",
    )(yg, ys, x2d, wo, n2w, wr_t, rb)


def _first_argmax(vals, iota, size):
    m = jnp.max(vals, axis=0, keepdims=True)
    idx = jnp.min(jnp.where(vals == m, iota, size), axis=0, keepdims=True)
    return m, idx


def _route_kernel(lt_ref, idx_ref, gate_ref, cnt_ref, carry_sc, *, tr):
    @pl.when(pl.program_id(0) == 0)
    def _():
        carry_sc[...] = jnp.zeros_like(carry_sc)

    iota8 = lax.broadcasted_iota(I32, (EXPERTS_PER_GROUP, tr), 0)
    gl = lt_ref[0:N_GROUPS, :]
    gmax, gidx = _first_argmax(gl, iota8, N_GROUPS)
    g_w = 1.0 / jnp.sum(jnp.exp(gl - gmax), axis=0, keepdims=True)

    ig = jnp.zeros((EXPERTS_PER_GROUP, tr), F32)
    for g in range(N_GROUPS):
        lo = N_GROUPS + g * EXPERTS_PER_GROUP
        ig = jnp.where(gidx == g, lt_ref[lo:lo + EXPERTS_PER_GROUP, :], ig)
    pe = jnp.exp(ig - jnp.max(ig, axis=0, keepdims=True))
    p = pe / jnp.sum(pe, axis=0, keepdims=True)
    p1, i1 = _first_argmax(p, iota8, EXPERTS_PER_GROUP)
    p2, i2 = _first_argmax(jnp.where(iota8 == i1, -1.0, p), iota8, EXPERTS_PER_GROUP)
    denom = p1 + p2
    e1 = gidx * EXPERTS_PER_GROUP + i1
    e2 = gidx * EXPERTS_PER_GROUP + i2

    iota_e = lax.broadcasted_iota(I32, (N_EXPERTS, tr), 0)
    oh1 = (iota_e == e1).astype(F32)
    oh2 = (iota_e == e2).astype(F32)
    ohs = oh1 + oh2
    r = lax.broadcasted_iota(I32, (tr, tr), 0)
    c = lax.broadcasted_iota(I32, (tr, tr), 1)
    before = jnp.where(r < c, 1.0, 0.0).astype(BF16)
    base = carry_sc[:, 0:1] + _dot(ohs.astype(BF16), before)
    rank1 = jnp.sum(oh1 * base, axis=0, keepdims=True)
    rank2 = jnp.sum(oh2 * base, axis=0, keepdims=True)
    carry_sc[...] = carry_sc[...] + jnp.sum(ohs, axis=1, keepdims=True)
    cnt_ref[...] = carry_sc[...]

    idx_ref[...] = jnp.zeros_like(idx_ref)
    idx_ref[0:1, :] = e1
    idx_ref[1:2, :] = e2
    idx_ref[2:3, :] = rank1.astype(I32)
    idx_ref[3:4, :] = rank2.astype(I32)
    gate_ref[...] = jnp.zeros_like(gate_ref)
    gate_ref[0:1, :] = g_w * p1 / denom
    gate_ref[1:2, :] = g_w * p2 / denom


def _route_call(lt, tr):
    n_tok = lt.shape[1]
    return pl.pallas_call(
        functools.partial(_route_kernel, tr=tr),
        grid=(n_tok // tr,),
        in_specs=[pl.BlockSpec((LANES, tr), lambda i: (0, i))],
        out_specs=[
            pl.BlockSpec((SUBLANES, tr), lambda i: (0, i)),
            pl.BlockSpec((SUBLANES, tr), lambda i: (0, i)),
            pl.BlockSpec((N_EXPERTS, LANES), lambda i: (0, 0)),
        ],
        out_shape=[
            jax.ShapeDtypeStruct((SUBLANES, n_tok), I32),
            jax.ShapeDtypeStruct((SUBLANES, n_tok), F32),
            jax.ShapeDtypeStruct((N_EXPERTS, LANES), F32),
        ],
        scratch_shapes=[pltpu.VMEM((N_EXPERTS, LANES), F32)],
        compiler_params=_params("arbitrary"),
        name="moe_route",
    )(lt)


DISPATCH_SLOTS = 3


def _dispatch_kernel(zf_ref, d1_ref, d2_ref, d1p_ref, d2p_ref, hp_ref, xs_ref,
                     stage, zbuf, sem_in, sem_out, sem_z, *, tb, bm):
    i = pl.program_id(0)
    last = pl.num_programs(0) - 1

    def block_load(step):
        s = step % DISPATCH_SLOTS
        return pltpu.make_async_copy(hp_ref.at[pl.ds(step * tb, tb)], stage.at[s], sem_in.at[s])

    def row_copies(step, da_ref, db_ref):
        s = step % DISPATCH_SLOTS
        return [pltpu.make_async_copy(stage.at[s, pl.ds(u, 1)],
                                      xs_ref.at[pl.ds(d_ref[0, 0, u], 1)], sem_out.at[s])
                for u in range(tb) for d_ref in (da_ref, db_ref)]

    def zero_fill(b):
        return pltpu.make_async_copy(zbuf, xs_ref.at[pl.ds(pl.multiple_of(b * bm, bm), bm)], sem_z)

    @pl.when(i == 0)
    def _():
        block_load(i).start()
        zbuf[...] = jnp.zeros_like(zbuf)

        def start(b, carry):
            @pl.when(zf_ref[b] != 0)
            def _():
                zero_fill(b).start()
            return carry

        def wait(b, carry):
            @pl.when(zf_ref[b] != 0)
            def _():
                zero_fill(b).wait()
            return carry

        lax.fori_loop(0, zf_ref.shape[0], start, 0)
        lax.fori_loop(0, zf_ref.shape[0], wait, 0)

    @pl.when(i < last)
    def _():
        block_load(i + 1).start()

    block_load(i).wait()
    for cp in row_copies(i, d1_ref, d2_ref):
        cp.start()

    @pl.when(i > 0)
    def _():
        for cp in row_copies(i - 1, d1p_ref, d2p_ref):
            cp.wait()

    @pl.when(i == last)
    def _():
        for cp in row_copies(i, d1_ref, d2_ref):
            cp.wait()


def _dispatch_call(zero_flag, d1, d2, hp, n_rows, tb, bm):
    n_tok, width = hp.shape
    cur = pl.BlockSpec((1, 1, tb), lambda i, zf: (i, 0, 0), memory_space=pltpu.SMEM)
    prev = pl.BlockSpec((1, 1, tb), lambda i, zf: (jnp.maximum(i - 1, 0), 0, 0),
                        memory_space=pltpu.SMEM)
    d1 = d1.reshape(n_tok // tb, 1, tb)
    d2 = d2.reshape(n_tok // tb, 1, tb)
    grid_spec = pltpu.PrefetchScalarGridSpec(
        num_scalar_prefetch=1,
        grid=(n_tok // tb,),
        in_specs=[cur, cur, prev, prev, pl.BlockSpec(memory_space=pl.ANY)],
        out_specs=pl.BlockSpec(memory_space=pl.ANY),
        scratch_shapes=[
            pltpu.VMEM((DISPATCH_SLOTS, tb, width), U32),
            pltpu.VMEM((bm, width), U32),
            pltpu.SemaphoreType.DMA((DISPATCH_SLOTS,)),
            pltpu.SemaphoreType.DMA((DISPATCH_SLOTS,)),
            pltpu.SemaphoreType.DMA(()),
        ],
    )
    return pl.pallas_call(
        functools.partial(_dispatch_kernel, tb=tb, bm=bm),
        grid_spec=grid_spec,
        out_shape=jax.ShapeDtypeStruct((n_rows, width), U32),
        compiler_params=_params("arbitrary"),
        name="moe_dispatch",
    )(zero_flag, d1, d2, d1, d2, hp)


def _expert_kernel(be_ref, nu_ref, first_ref, nxt_ref, slot_ref, xs_ref, wg_hbm, wu_hbm, wd_hbm,
                   ys_ref, wg32, wu32, wd32, wgb, wub, wdb, sem):
    i = pl.program_id(0)
    used = i < nu_ref[0]

    def weight_copies(e, s):
        return [pltpu.make_async_copy(src.at[e], dst.at[s], sem.at[s])
                for src, dst in ((wg_hbm, wg32), (wu_hbm, wu32), (wd_hbm, wd32))]

    @pl.when(used & (first_ref[i] != 0))
    def _():
        e = be_ref[i]
        s = slot_ref[i]

        @pl.when(i == 0)
        def _():
            for cp in weight_copies(e, s):
                cp.start()

        for cp in weight_copies(e, s):
            cp.wait()

        @pl.when(nxt_ref[i] >= 0)
        def _():
            for cp in weight_copies(nxt_ref[i], 1 - s):
                cp.start()

        wgb[...] = wg32[s].astype(BF16)
        wub[...] = wu32[s].astype(BF16)
        wdb[...] = wd32[s].astype(BF16)

    @pl.when(used)
    def _():
        hi, lo = _unpack_bf16_pair(xs_ref[...])
        xb = jnp.concatenate([hi, lo], axis=1)
        hid = _silu(_dot(xb, wgb[...])) * _dot(xb, wub[...])
        ys_ref[...] = _dot(hid.astype(BF16), wdb[...])

    @pl.when(jnp.logical_not(used))
    def _():
        ys_ref[...] = jnp.zeros_like(ys_ref)


def _expert_call(block_expert, n_used, first, nxt, slot, xs, wg, wu, wd, bm):
    n_rows, width = xs.shape
    d = 2 * width
    nb = n_rows // bm
    grid_spec = pltpu.PrefetchScalarGridSpec(
        num_scalar_prefetch=5,
        grid=(nb,),
        in_specs=[
            pl.BlockSpec((bm, width), lambda i, be, nu, *_: (jnp.minimum(i, nu[0] - 1), 0)),
            pl.BlockSpec(memory_space=pl.ANY),
            pl.BlockSpec(memory_space=pl.ANY),
            pl.BlockSpec(memory_space=pl.ANY),
        ],
        out_specs=pl.BlockSpec((bm, d), lambda i, *_: (i, 0)),
        scratch_shapes=[
            pltpu.VMEM((2, d, D_EXPERT), F32),
            pltpu.VMEM((2, d, D_EXPERT), F32),
            pltpu.VMEM((2, D_EXPERT, d), F32),
            pltpu.VMEM((d, D_EXPERT), BF16),
            pltpu.VMEM((d, D_EXPERT), BF16),
            pltpu.VMEM((D_EXPERT, d), BF16),
            pltpu.SemaphoreType.DMA((2,)),
        ],
    )
    return pl.pallas_call(
        _expert_kernel,
        grid_spec=grid_spec,
        out_shape=jax.ShapeDtypeStruct((n_rows, d), F32),
        compiler_params=_params("arbitrary"),
        name="moe_experts",
    )(block_expert, n_used, first, nxt, slot, xs, wg, wu, wd)


def _combine_kernel(d1_ref, d2_ref, d1n_ref, d2n_ref, gate_ref, h1_ref, wf_ref, ys_ref, out_ref,
                    buf, sem, *, tc):
    i = pl.program_id(0)
    slot = i % 2

    def row_copies(da_ref, db_ref, s):
        return [pltpu.make_async_copy(ys_ref.at[pl.ds(d_ref[0, 0, u], 1)],
                                      buf.at[s, k, pl.ds(u, 1)], sem.at[s])
                for u in range(tc) for k, d_ref in enumerate((da_ref, db_ref))]

    @pl.when(i == 0)
    def _():
        for cp in row_copies(d1_ref, d2_ref, 0):
            cp.start()

    @pl.when(i + 1 < pl.num_programs(0))
    def _():
        for cp in row_copies(d1n_ref, d2n_ref, 1 - slot):
            cp.start()

    for cp in row_copies(d1_ref, d2_ref, slot):
        cp.wait()
    gates = gate_ref[...]
    h = h1_ref[...] + buf[slot, 0] * gates[:, 0:1] + buf[slot, 1] * gates[:, 1:2]
    out_ref[...] = _rms_scale(h) * wf_ref[...]


def _combine_call(d1, d2, gates, h1, wf, ys, tc):
    n_tok, d = h1.shape
    steps = n_tok // tc
    cur = pl.BlockSpec((1, 1, tc), lambda i: (i, 0, 0), memory_space=pltpu.SMEM)
    nxt = pl.BlockSpec((1, 1, tc), lambda i: (jnp.minimum(i + 1, steps - 1), 0, 0),
                       memory_space=pltpu.SMEM)
    d1 = d1.reshape(steps, 1, tc)
    d2 = d2.reshape(steps, 1, tc)
    return pl.pallas_call(
        functools.partial(_combine_kernel, tc=tc),
        grid=(steps,),
        in_specs=[
            cur, cur, nxt, nxt,
            pl.BlockSpec((tc, LANES), lambda i: (i, 0)),
            pl.BlockSpec((tc, d), lambda i: (i, 0)),
            _resident((1, d)),
            pl.BlockSpec(memory_space=pl.ANY),
        ],
        out_specs=pl.BlockSpec((tc, d), lambda i: (i, 0)),
        out_shape=jax.ShapeDtypeStruct((n_tok, d), F32),
        scratch_shapes=[
            pltpu.VMEM((2, TOP_K, tc, d), F32),
            pltpu.SemaphoreType.DMA((2,)),
        ],
        compiler_params=_params("arbitrary"),
        name="moe_combine",
    )(d1, d2, d1, d2, gates, h1, wf, ys)


def _tiles(seq, n_tok):
    return dict(
        proj=min(256, seq),
        gdn=min(512, seq),
        outp=min(512, n_tok),
        route=min(512, n_tok),
        disp=min(256, n_tok),
        bm=256,
        comb=min(256, n_tok),
    )


def _layer(h, n1w, w_in, conv_qkv_w, a_log, dt_bias, gdn_norm_w, conv_sc_w, w_out,
           n2w, wg_r, bg_r, we_r, be_r, w_gate, w_up, w_down, final_w, last):
    bsz, seq, d = h.shape
    n_tok = bsz * seq
    ts = _tiles(seq, n_tok)

    qkvz_cols = 4 * GDN_WIDTH
    w_qkvz = w_in[:, :qkvz_cols].astype(BF16)
    w_ba = jnp.pad(w_in[:, qkvz_cols:qkvz_cols + 2 * GDN_HEADS],
                   ((0, 0), (0, LANES - 2 * GDN_HEADS))).astype(BF16)
    sc0 = qkvz_cols + 2 * GDN_HEADS
    w_sc = w_in[:, sc0:].reshape(d, 3, SC_WIDTH // SC_TILE, SC_TILE)
    w_sc = jnp.swapaxes(w_sc, 1, 2).reshape(d, 3 * SC_WIDTH).astype(BF16)
    pad8 = (GDN_HEADS, LANES - 2 * GDN_HEADS)
    alog_v = jnp.pad(a_log, pad8).reshape(1, LANES)
    dtb_v = jnp.pad(dt_bias, pad8).reshape(1, LANES)

    qkvz, y_sc, bg = _inproj_call(h, n1w.reshape(1, d), jnp.concatenate([w_qkvz, w_sc], axis=1),
                                  w_ba, conv_qkv_w, conv_sc_w, alog_v, dtb_v, ts["proj"])

    gt = jnp.swapaxes(bg[:, :, :2 * GDN_HEADS].reshape(bsz, seq // CHUNK, CHUNK, 2 * GDN_HEADS),
                      2, 3)
    y_gdn = _gdn_call(qkvz, bg, gt, gdn_norm_w.reshape(1, HEAD_DIM), ts["gdn"])

    n_router = N_GROUPS + N_EXPERTS
    wr = jnp.pad(jnp.concatenate([wg_r, we_r], axis=1), ((0, 0), (0, LANES - n_router)))
    wr_hi = wr.astype(BF16)
    wr_t = jnp.concatenate([wr_hi, (wr - wr_hi.astype(F32)).astype(BF16)], axis=1)
    rb = jnp.pad(jnp.concatenate([bg_r, be_r]), (0, LANES - n_router)).reshape(LANES, 1)
    x2d = h.reshape(n_tok, d)
    h1, hp, lt = _outproj_call(y_gdn.reshape(n_tok, GDN_WIDTH), y_sc.reshape(n_tok, SC_WIDTH),
                               x2d, w_out.astype(BF16), n2w.reshape(1, d), wr_t, rb, ts["outp"])

    idx, gates_t, cnt = _route_call(lt, ts["route"])
    bm = ts["bm"]
    counts = cnt[:, 0].astype(I32)
    padded = (counts + bm - 1) // bm * bm
    pad_end = jnp.cumsum(padded)
    pad_start = pad_end - padded
    d1 = pad_start[idx[0]] + idx[2]
    d2 = pad_start[idx[1]] + idx[3]
    nb = n_tok * TOP_K // bm + N_EXPERTS
    block_row = jnp.arange(nb, dtype=I32) * bm
    block_expert = jnp.minimum(
        jnp.sum((pad_end[None, :] <= block_row[:, None]).astype(I32), axis=1), N_EXPERTS - 1)
    n_used = (pad_end[-1:] // bm).astype(I32)
    gates = jnp.pad(gates_t[:TOP_K].T, ((0, 0), (0, LANES - TOP_K)))

    has_rows = padded > 0
    e_ids = jnp.arange(N_EXPERTS, dtype=I32)
    later = jnp.where(has_rows, e_ids, N_EXPERTS)
    next_used = jnp.concatenate([lax.cummin(later, reverse=True)[1:],
                                 jnp.full((1,), N_EXPERTS, I32)])
    next_used = jnp.where(next_used < N_EXPERTS, next_used, -1)
    rank_used = jnp.cumsum(has_rows.astype(I32)) - 1
    first = (block_row == pad_start[block_expert]).astype(I32)
    nxt = next_used[block_expert]
    slot = rank_used[block_expert] % 2
    is_last = jnp.any((pad_end[None, :] == (block_row + bm)[:, None]) & has_rows[None, :], axis=1)
    zero_flag = (is_last | (block_row >= pad_end[-1])).astype(I32)

    xs = _dispatch_call(zero_flag, d1, d2, hp, nb * bm, ts["disp"], bm)
    ys = _expert_call(block_expert, n_used, first, nxt, slot, xs, w_gate, w_up, w_down, bm)
    out = _combine_call(d1, d2, gates, h1, final_w.reshape(1, d), ys, ts["comb"])
    del last
    return out.reshape(bsz, seq, d)


def kernel(x, norm1_w, w_in, conv_qkv_w, a_log, dt_bias, gdn_norm_w, conv_sc_w, w_out, norm2_w,
           router_group_w, router_group_b, router_expert_w, router_expert_b, w_gate, w_up,
           w_down, final_norm_w):
    depth = norm1_w.shape[0]
    assert depth == 1, "the combine call applies the final RMSNorm, so a single layer is assumed"
    return _layer(x, norm1_w[0], w_in[0], conv_qkv_w[0], a_log[0], dt_bias[0], gdn_norm_w[0],
                  conv_sc_w[0], w_out[0], norm2_w[0], router_group_w[0], router_group_b[0],
                  router_expert_w[0], router_expert_b[0], w_gate[0], w_up[0], w_down[0],
                  final_norm_w, True)
```

```python
import functools

import jax
import jax.numpy as jnp
from jax import lax
from jax.experimental import pallas as pl
from jax.experimental.pallas import tpu as pltpu

F32 = jnp.float32
BF16 = jnp.bfloat16
I32 = jnp.int32
U32 = jnp.uint32
HIGHEST = lax.Precision.HIGHEST

EPS = 1e-6
D_MODEL = 2048
HEAD_DIM = 128
GDN_HEADS = 8
GDN_WIDTH = GDN_HEADS * HEAD_DIM
GDN_CONV = 4
CHUNK = 64
SC_WIDTH = 1024
SC_CONV = 3
N_GROUPS = 8
EXPERTS_PER_GROUP = 8
N_EXPERTS = N_GROUPS * EXPERTS_PER_GROUP
TOP_K = 2
D_EXPERT = 512
LANES = 128
SUBLANES = 8
SC_TILE = 256
VMEM_LIMIT = 56 * 1024 * 1024


def _sigmoid(x):
    return 1.0 / (1.0 + jnp.exp(-x))


def _silu(x):
    return x * _sigmoid(x)


def _softplus(x):
    return jnp.maximum(x, 0.0) + jnp.log1p(jnp.exp(-jnp.abs(x)))


def _rms_scale(x):
    return x * lax.rsqrt(jnp.mean(x * x, axis=-1, keepdims=True) + EPS)


def _dot(a, b):
    return jnp.dot(a, b, preferred_element_type=F32)


def _hdot(a, b):
    return jnp.dot(a, b, preferred_element_type=F32, precision=HIGHEST)


def _params(*sem):
    return pltpu.CompilerParams(dimension_semantics=sem, vmem_limit_bytes=VMEM_LIMIT)


ROW_PIECE = 64


def _conv_piece(x, prev, cw):
    taps = cw.shape[0]
    rows = x.shape[0]
    ext = jnp.concatenate([prev, x], axis=0)
    y = cw[taps - 1:taps, :] * x
    for s in range(1, taps):
        shifted = pltpu.roll(ext, s, axis=0)[SUBLANES:SUBLANES + rows, :]
        y = y + cw[taps - 1 - s:taps - s, :] * shifted
    return y


def _resident(shape):
    return pl.BlockSpec(shape, lambda *_: (0,) * len(shape), pipeline_mode=pl.Buffered(1))


SC_COL0 = 4 * GDN_WIDTH
PROJ_ORDER = (("qkvz", 0), ("sc", 0), ("qkvz", 1), ("sc", 1), ("qkvz", 2), ("sc", 2),
              ("qkvz", 3), ("sc", 3))


def _inproj_kernel(x_ref, n1w_ref, w_ref, wba_ref, cwq_ref, cws_ref, alog_ref, dtb_ref, ltri_ref,
                   qkvz_ref, ysc_ref, bg_ref, carry_q, carry_s, *, tm):
    @pl.when(pl.program_id(1) == 0)
    def _():
        carry_q[...] = jnp.zeros_like(carry_q)
        carry_s[...] = jnp.zeros_like(carry_s)

    hn = (_rms_scale(x_ref[...]) * n1w_ref[...]).astype(BF16)

    ba = _dot(hn, wba_ref[...])
    lane = lax.broadcasted_iota(I32, ba.shape, 1)
    g = -jnp.exp(alog_ref[...]) * _softplus(ba + dtb_ref[...])
    g = jnp.where((lane >= GDN_HEADS) & (lane < 2 * GDN_HEADS), g, 0.0)
    gc = _hdot(ltri_ref[...], g)
    bg_ref[...] = jnp.where(lane < GDN_HEADS, _sigmoid(ba), gc)

    def epilogue(kind, j, p):
        pieces = [(r * ROW_PIECE, c * LANES) for r in range(tm // ROW_PIECE)
                  for c in range((SC_TILE if kind == "sc" else GDN_WIDTH) // LANES)]
        tails = {}
        for r0, c0 in pieces:
            rows = slice(r0, r0 + ROW_PIECE)
            lanes = slice(c0, c0 + LANES)
            if kind == "sc":
                out_cols = slice(SC_TILE * j + c0, SC_TILE * j + c0 + LANES)
                x = p[rows, SC_TILE + c0:SC_TILE + c0 + LANES] * p[
                    rows, 2 * SC_TILE + c0:2 * SC_TILE + c0 + LANES]
                prev = tails.get(c0, carry_s[j, :, lanes]) if r0 else carry_s[j, :, lanes]
                tails[c0] = x[ROW_PIECE - SUBLANES:, :]
                y = _conv_piece(x, prev, cws_ref[:, out_cols])
                ysc_ref[rows, out_cols] = (p[rows, lanes] * y).astype(BF16)
                continue
            out_cols = slice(j * GDN_WIDTH + c0, j * GDN_WIDTH + c0 + LANES)
            x = p[rows, lanes]
            if j == 3:
                qkvz_ref[rows, out_cols] = x.astype(BF16)
                continue
            prev = tails.get(c0, carry_q[j, :, lanes]) if r0 else carry_q[j, :, lanes]
            tails[c0] = x[ROW_PIECE - SUBLANES:, :]
            y = _silu(_conv_piece(x, prev, cwq_ref[:, out_cols]))
            if j < 2:
                scale = HEAD_DIM ** -0.5 if j == 0 else 1.0
                y = y * (lax.rsqrt(jnp.sum(y * y, axis=-1, keepdims=True) + EPS) * scale)
            qkvz_ref[rows, out_cols] = y.astype(BF16)
        carry = carry_s if kind == "sc" else carry_q
        for c0, tail in tails.items():
            carry[j, :, c0:c0 + LANES] = tail

    p_all = _dot(hn, w_ref[...])
    for kind, j in PROJ_ORDER:
        if kind == "qkvz":
            epilogue(kind, j, p_all[:, j * GDN_WIDTH:(j + 1) * GDN_WIDTH])
        else:
            lo = SC_COL0 + 3 * SC_TILE * j
            epilogue(kind, j, p_all[:, lo:lo + 3 * SC_TILE])


def _inproj_call(x, n1w, w, wba, cwq, cws, alog, dtb, tm):
    bsz, seq, d = x.shape
    idx = jnp.arange(tm)
    ltri = ((idx[:, None] >= idx[None, :]) &
            (idx[:, None] // CHUNK == idx[None, :] // CHUNK)).astype(F32)

    def rows(width):
        return pl.BlockSpec((None, tm, width), lambda b, i: (b, i, 0))

    return pl.pallas_call(
        functools.partial(_inproj_kernel, tm=tm),
        grid=(bsz, seq // tm),
        in_specs=[
            rows(d),
            _resident((1, d)),
            _resident((d, SC_COL0 + 3 * SC_WIDTH)),
            _resident((d, LANES)),
            _resident((GDN_CONV, 3 * GDN_WIDTH)),
            _resident((SC_CONV, SC_WIDTH)),
            _resident((1, LANES)),
            _resident((1, LANES)),
            _resident((tm, tm)),
        ],
        out_specs=[rows(4 * GDN_WIDTH), rows(SC_WIDTH), rows(LANES)],
        out_shape=[
            jax.ShapeDtypeStruct((bsz, seq, 4 * GDN_WIDTH), BF16),
            jax.ShapeDtypeStruct((bsz, seq, SC_WIDTH), BF16),
            jax.ShapeDtypeStruct((bsz, seq, LANES), F32),
        ],
        scratch_shapes=[
            pltpu.VMEM((3, SUBLANES, GDN_WIDTH), F32),
            pltpu.VMEM((SC_WIDTH // SC_TILE, SUBLANES, SC_TILE), F32),
        ],
        compiler_params=_params("arbitrary", "arbitrary"),
        name="in_proj",
    )(x, n1w, w, wba, cwq, cws, alog, dtb, ltri)


GDN_CHUNKS_PER_TRIP = 4


def _sdot(a, b):
    return _dot(a.astype(BF16), b.astype(BF16))


def _gdn_kernel(q_ref, k_ref, v_ref, z_ref, bg_ref, gt_ref, nw_ref, out_ref, state_sc, *, tc):
    @pl.when(pl.program_id(1) == 0)
    def _():
        state_sc[...] = jnp.zeros_like(state_sc)

    row = lax.broadcasted_iota(I32, (CHUNK, CHUNK), 0)
    col = lax.broadcasted_iota(I32, (CHUNK, CHUNK), 1)
    causal = row >= col
    strict = row > col
    eye = (row == col).astype(F32)
    same16 = (row // 16) == (col // 16)
    same32 = (row // 32) == (col // 32)
    m16 = same16.astype(F32)
    m32 = (same32 & jnp.logical_not(same16)).astype(F32)
    m64 = jnp.logical_not(same32).astype(F32)
    nw = nw_ref[...]
    tt = (((0,), (0,)), ((), ()))
    nt = (((1,), (1,)), ((), ()))
    heads = range(GDN_HEADS)

    def hcols(h):
        return slice(h * HEAD_DIM, (h + 1) * HEAD_DIM)

    def trip_body(t, carry):
        units = []
        for cc in range(GDN_CHUNKS_PER_TRIP):
            c = t * GDN_CHUNKS_PER_TRIP + cc
            rows = pl.ds(pl.multiple_of(c * CHUNK, CHUNK), CHUNK)
            bgc = bg_ref[rows, :]
            gt = gt_ref[c]
            for h in heads:
                gc = bgc[:, GDN_HEADS + h:GDN_HEADS + h + 1]
                units.append(dict(
                    rows=rows, h=h, gc=gc,
                    beta=bgc[:, h:h + 1],
                    gcr=gt[GDN_HEADS + h:GDN_HEADS + h + 1, :],
                    glast=gc[CHUNK - 1:CHUNK, :],
                    q=q_ref[rows, hcols(h)], k=k_ref[rows, hcols(h)]))

        for u in units:
            u["kq"] = lax.dot_general(jnp.concatenate([u["k"], u["q"]], axis=0), u["k"], nt,
                                      preferred_element_type=F32)
        for u in units:
            decay = jnp.where(causal, jnp.exp(u["gc"] - u["gcr"]), 0.0)
            u["a"] = jnp.where(strict, u["kq"][:CHUNK] * u["beta"] * decay, 0.0)
            u["attn"] = (u["kq"][CHUNK:] * decay).astype(BF16)

        for u in units:
            u["n"] = -(u["a"] * m16)
            u["p"] = eye + u["n"]
        for _ in range(3):
            for u in units:
                u["n"] = _sdot(u["n"], u["n"])
            for u in units:
                u["p"] = u["p"] + _sdot(u["p"], u["n"])
        for m in (m32, m64):
            for u in units:
                u["e"] = _sdot(u["a"] * m, u["p"])
            for u in units:
                u["p"] = u["p"] - _sdot(u["p"], u["e"])

        for u in units:
            u["eg"] = jnp.exp(u["gc"])
            u["kf"] = u["k"].astype(F32)
            v = v_ref[u["rows"], hcols(u["h"])].astype(F32)
            bw = u["beta"] * u["eg"]
            rhs = jnp.concatenate([v * u["beta"], u["kf"] * bw], axis=1)
            u["uw"] = _sdot(u["p"], rhs)

        state = [state_sc[h] for h in heads]
        for cc in range(GDN_CHUNKS_PER_TRIP):
            us = units[cc * GDN_HEADS:(cc + 1) * GDN_HEADS]
            for u in us:
                wq = jnp.concatenate([u["uw"][:, HEAD_DIM:], u["q"].astype(F32) * u["eg"]], axis=0)
                u["rs"] = _dot(wq.astype(BF16), state[u["h"]].astype(BF16))
            for u in us:
                u["vb"] = (u["uw"][:, :HEAD_DIM] - u["rs"][:CHUNK]).astype(BF16)
                u["o"] = u["rs"][CHUNK:] + _dot(u["attn"], u["vb"])
            for u in us:
                k_dec = (u["kf"] * jnp.exp(u["glast"] - u["gc"])).astype(BF16)
                state[u["h"]] = state[u["h"]] * jnp.exp(u["glast"]) + lax.dot_general(
                    k_dec, u["vb"], tt, preferred_element_type=F32)
            for u in us:
                z = z_ref[u["rows"], hcols(u["h"])].astype(F32)
                out_ref[u["rows"], hcols(u["h"])] = (
                    _rms_scale(u["o"]) * nw * _silu(z)).astype(BF16)
        for h in heads:
            state_sc[h] = state[h]
        return carry

    lax.fori_loop(0, tc // (CHUNK * GDN_CHUNKS_PER_TRIP), trip_body, 0)


def _gdn_call(qkvz, bg, gt, nw, tc):
    bsz, seq, _ = qkvz.shape

    def col(j):
        return pl.BlockSpec((None, tc, GDN_WIDTH), lambda b, i: (b, i, j))

    return pl.pallas_call(
        functools.partial(_gdn_kernel, tc=tc),
        grid=(bsz, seq // tc),
        in_specs=[
            col(0), col(1), col(2), col(3),
            pl.BlockSpec((None, tc, LANES), lambda b, i: (b, i, 0)),
            pl.BlockSpec((None, tc // CHUNK, 2 * GDN_HEADS, CHUNK), lambda b, i: (b, i, 0, 0)),
            _resident((1, HEAD_DIM)),
        ],
        out_specs=pl.BlockSpec((None, tc, GDN_WIDTH), lambda b, i: (b, i, 0)),
        out_shape=jax.ShapeDtypeStruct((bsz, seq, GDN_WIDTH), BF16),
        scratch_shapes=[pltpu.VMEM((GDN_HEADS, HEAD_DIM, HEAD_DIM), F32)],
        compiler_params=_params("arbitrary", "arbitrary"),
        name="gated_delta_rule",
    )(qkvz, qkvz, qkvz, qkvz, bg, gt, nw)


def _pack_bf16_pair(hi, lo):
    hb = lax.bitcast_convert_type(hi.astype(BF16).astype(F32), U32)
    lb = lax.bitcast_convert_type(lo.astype(BF16).astype(F32), U32)
    return (hb & jnp.uint32(0xFFFF0000)) | lax.shift_right_logical(lb, jnp.uint32(16))


def _unpack_bf16_pair(p):
    hi = lax.bitcast_convert_type(p & jnp.uint32(0xFFFF0000), F32).astype(BF16)
    lo = lax.bitcast_convert_type(lax.shift_left(p, jnp.uint32(16)), F32).astype(BF16)
    return hi, lo


def _outproj_kernel(yg_ref, ys_ref, x_ref, wo_ref, n2w_ref, wr_ref, rb_ref,
                    h1_ref, hp_ref, lt_ref):
    h1 = (x_ref[...] + _dot(yg_ref[...], wo_ref[0:GDN_WIDTH, :])
          + _dot(ys_ref[...], wo_ref[GDN_WIDTH:GDN_WIDTH + SC_WIDTH, :]))
    h1_ref[...] = h1
    hn = _rms_scale(h1) * n2w_ref[...]
    hn_hi = hn.astype(BF16)
    hn_lo = (hn - hn_hi.astype(F32)).astype(BF16)
    lg = _dot(hn_hi, wr_ref[...])
    lg = lg[:, :LANES] + lg[:, LANES:] + _dot(hn_lo, wr_ref[:, :LANES])
    lt_ref[...] = lg.T + rb_ref[...]
    half = D_MODEL // 2
    hp_ref[...] = _pack_bf16_pair(hn[:, :half], hn[:, half:])


def _outproj_call(yg, ys, x2d, wo, n2w, wr_t, rb, tm):
    n_tok, d = x2d.shape
    return pl.pallas_call(
        _outproj_kernel,
        grid=(n_tok // tm,),
        in_specs=[
            pl.BlockSpec((tm, GDN_WIDTH), lambda i: (i, 0)),
            pl.BlockSpec((tm, SC_WIDTH), lambda i: (i, 0)),
            pl.BlockSpec((tm, d), lambda i: (i, 0)),
            _resident((GDN_WIDTH + SC_WIDTH, d)),
            _resident((1, d)),
            _resident((d, 2 * LANES)),
            _resident((LANES, 1)),
        ],
        out_specs=[
            pl.BlockSpec((tm, d), lambda i: (i, 0)),
            pl.BlockSpec((tm, d // 2), lambda i: (i, 0)),
            pl.BlockSpec((LANES, tm), lambda i: (0, i)),
        ],
        out_shape=[
            jax.ShapeDtypeStruct((n_tok, d), F32),
            jax.ShapeDtypeStruct((n_tok, d // 2), U32),
            jax.ShapeDtypeStruct((LANES, n_tok), F32),
        ],
        compiler_params=_params("arbitrary"),
        name="out_proj_router",
    )(yg, ys, x2d, wo, n2w, wr_t, rb)


def _first_argmax(vals, iota, size):
    m = jnp.max(vals, axis=0, keepdims=True)
    idx = jnp.min(jnp.where(vals == m, iota, size), axis=0, keepdims=True)
    return m, idx


def _route_kernel(lt_ref, idx_ref, gate_ref, cnt_ref, carry_sc, *, tr):
    @pl.when(pl.program_id(0) == 0)
    def _():
        carry_sc[...] = jnp.zeros_like(carry_sc)

    iota8 = lax.broadcasted_iota(I32, (EXPERTS_PER_GROUP, tr), 0)
    gl = lt_ref[0:N_GROUPS, :]
    gmax, gidx = _first_argmax(gl, iota8, N_GROUPS)
    g_w = 1.0 / jnp.sum(jnp.exp(gl - gmax), axis=0, keepdims=True)

    ig = jnp.zeros((EXPERTS_PER_GROUP, tr), F32)
    for g in range(N_GROUPS):
        lo = N_GROUPS + g * EXPERTS_PER_GROUP
        ig = jnp.where(gidx == g, lt_ref[lo:lo + EXPERTS_PER_GROUP, :], ig)
    pe = jnp.exp(ig - jnp.max(ig, axis=0, keepdims=True))
    p = pe / jnp.sum(pe, axis=0, keepdims=True)
    p1, i1 = _first_argmax(p, iota8, EXPERTS_PER_GROUP)
    p2, i2 = _first_argmax(jnp.where(iota8 == i1, -1.0, p), iota8, EXPERTS_PER_GROUP)
    denom = p1 + p2
    e1 = gidx * EXPERTS_PER_GROUP + i1
    e2 = gidx * EXPERTS_PER_GROUP + i2

    iota_e = lax.broadcasted_iota(I32, (N_EXPERTS, tr), 0)
    oh1 = (iota_e == e1).astype(F32)
    oh2 = (iota_e == e2).astype(F32)
    ohs = oh1 + oh2
    r = lax.broadcasted_iota(I32, (tr, tr), 0)
    c = lax.broadcasted_iota(I32, (tr, tr), 1)
    before = jnp.where(r < c, 1.0, 0.0).astype(BF16)
    base = carry_sc[:, 0:1] + _dot(ohs.astype(BF16), before)
    rank1 = jnp.sum(oh1 * base, axis=0, keepdims=True)
    rank2 = jnp.sum(oh2 * base, axis=0, keepdims=True)
    carry_sc[...] = carry_sc[...] + jnp.sum(ohs, axis=1, keepdims=True)
    cnt_ref[...] = carry_sc[...]

    idx_ref[...] = jnp.zeros_like(idx_ref)
    idx_ref[0:1, :] = e1
    idx_ref[1:2, :] = e2
    idx_ref[2:3, :] = rank1.astype(I32)
    idx_ref[3:4, :] = rank2.astype(I32)
    sub = lax.broadcasted_iota(I32, (LANES, tr), 0)
    gmat = jnp.where(sub == 0, g_w * p1 / denom, jnp.where(sub == 1, g_w * p2 / denom, 0.0))
    gate_ref[...] = gmat.T


def _route_call(lt, tr):
    n_tok = lt.shape[1]
    return pl.pallas_call(
        functools.partial(_route_kernel, tr=tr),
        grid=(n_tok // tr,),
        in_specs=[pl.BlockSpec((LANES, tr), lambda i: (0, i))],
        out_specs=[
            pl.BlockSpec((SUBLANES, tr), lambda i: (0, i)),
            pl.BlockSpec((tr, LANES), lambda i: (i, 0)),
            pl.BlockSpec((N_EXPERTS, LANES), lambda i: (0, 0)),
        ],
        out_shape=[
            jax.ShapeDtypeStruct((SUBLANES, n_tok), I32),
            jax.ShapeDtypeStruct((n_tok, LANES), F32),
            jax.ShapeDtypeStruct((N_EXPERTS, LANES), F32),
        ],
        scratch_shapes=[pltpu.VMEM((N_EXPERTS, LANES), F32)],
        compiler_params=_params("arbitrary"),
        name="moe_route",
    )(lt)


DISPATCH_SLOTS = 3


def _dispatch_kernel(zf_ref, d1_ref, d2_ref, d1p_ref, d2p_ref, hp_ref, xs_ref,
                     stage, zbuf, sem_in, sem_out, sem_z, *, tb, bm):
    i = pl.program_id(0)
    last = pl.num_programs(0) - 1

    def block_load(step):
        s = step % DISPATCH_SLOTS
        return pltpu.make_async_copy(hp_ref.at[pl.ds(step * tb, tb)], stage.at[s], sem_in.at[s])

    def row_copies(step, da_ref, db_ref):
        s = step % DISPATCH_SLOTS
        return [pltpu.make_async_copy(stage.at[s, pl.ds(u, 1)],
                                      xs_ref.at[pl.ds(d_ref[0, 0, u], 1)], sem_out.at[s])
                for u in range(tb) for d_ref in (da_ref, db_ref)]

    def zero_fill(b):
        return pltpu.make_async_copy(zbuf, xs_ref.at[pl.ds(pl.multiple_of(b * bm, bm), bm)], sem_z)

    @pl.when(i == 0)
    def _():
        block_load(i).start()
        zbuf[...] = jnp.zeros_like(zbuf)

        def start(b, carry):
            @pl.when(zf_ref[b] != 0)
            def _():
                zero_fill(b).start()
            return carry

        def wait(b, carry):
            @pl.when(zf_ref[b] != 0)
            def _():
                zero_fill(b).wait()
            return carry

        lax.fori_loop(0, zf_ref.shape[0], start, 0)
        lax.fori_loop(0, zf_ref.shape[0], wait, 0)

    @pl.when(i < last)
    def _():
        block_load(i + 1).start()

    block_load(i).wait()
    for cp in row_copies(i, d1_ref, d2_ref):
        cp.start()

    @pl.when(i > 0)
    def _():
        for cp in row_copies(i - 1, d1p_ref, d2p_ref):
            cp.wait()

    @pl.when(i == last)
    def _():
        for cp in row_copies(i, d1_ref, d2_ref):
            cp.wait()


def _dispatch_call(zero_flag, d1, d2, hp, n_rows, tb, bm):
    n_tok, width = hp.shape
    cur = pl.BlockSpec((1, 1, tb), lambda i, zf: (i, 0, 0), memory_space=pltpu.SMEM)
    prev = pl.BlockSpec((1, 1, tb), lambda i, zf: (jnp.maximum(i - 1, 0), 0, 0),
                        memory_space=pltpu.SMEM)
    d1 = d1.reshape(n_tok // tb, 1, tb)
    d2 = d2.reshape(n_tok // tb, 1, tb)
    grid_spec = pltpu.PrefetchScalarGridSpec(
        num_scalar_prefetch=1,
        grid=(n_tok // tb,),
        in_specs=[cur, cur, prev, prev, pl.BlockSpec(memory_space=pl.ANY)],
        out_specs=pl.BlockSpec(memory_space=pl.ANY),
        scratch_shapes=[
            pltpu.VMEM((DISPATCH_SLOTS, tb, width), U32),
            pltpu.VMEM((bm, width), U32),
            pltpu.SemaphoreType.DMA((DISPATCH_SLOTS,)),
            pltpu.SemaphoreType.DMA((DISPATCH_SLOTS,)),
            pltpu.SemaphoreType.DMA(()),
        ],
    )
    return pl.pallas_call(
        functools.partial(_dispatch_kernel, tb=tb, bm=bm),
        grid_spec=grid_spec,
        out_shape=jax.ShapeDtypeStruct((n_rows, width), U32),
        compiler_params=_params("arbitrary"),
        name="moe_dispatch",
    )(zero_flag, d1, d2, d1, d2, hp)


def _expert_kernel(be_ref, nu_ref, first_ref, nxt_ref, slot_ref, xs_ref, wg_hbm, wu_hbm, wd_hbm,
                   ys_ref, wg32, wu32, wd32, wgb, wub, wdb, sem):
    i = pl.program_id(0)
    used = i < nu_ref[0]

    def weight_copies(e, s):
        return [pltpu.make_async_copy(src.at[e], dst.at[s], sem.at[s])
                for src, dst in ((wg_hbm, wg32), (wu_hbm, wu32), (wd_hbm, wd32))]

    @pl.when(used & (first_ref[i] != 0))
    def _():
        e = be_ref[i]
        s = slot_ref[i]

        @pl.when(i == 0)
        def _():
            for cp in weight_copies(e, s):
                cp.start()

        for cp in weight_copies(e, s):
            cp.wait()

        @pl.when(nxt_ref[i] >= 0)
        def _():
            for cp in weight_copies(nxt_ref[i], 1 - s):
                cp.start()

        wgb[...] = wg32[s].astype(BF16)
        wub[...] = wu32[s].astype(BF16)
        wdb[...] = wd32[s].astype(BF16)

    @pl.when(used)
    def _():
        hi, lo = _unpack_bf16_pair(xs_ref[...])
        xb = jnp.concatenate([hi, lo], axis=1)
        hid = _silu(_dot(xb, wgb[...])) * _dot(xb, wub[...])
        y = _dot(hid.astype(BF16), wdb[...])
        ys_ref[...] = _pack_bf16_pair(y[:, :D_MODEL // 2], y[:, D_MODEL // 2:])

    @pl.when(jnp.logical_not(used))
    def _():
        ys_ref[...] = jnp.zeros_like(ys_ref)


def _expert_call(block_expert, n_used, first, nxt, slot, xs, wg, wu, wd, bm):
    n_rows, width = xs.shape
    d = 2 * width
    nb = n_rows // bm
    grid_spec = pltpu.PrefetchScalarGridSpec(
        num_scalar_prefetch=5,
        grid=(nb,),
        in_specs=[
            pl.BlockSpec((bm, width), lambda i, be, nu, *_: (jnp.minimum(i, nu[0] - 1), 0)),
            pl.BlockSpec(memory_space=pl.ANY),
            pl.BlockSpec(memory_space=pl.ANY),
            pl.BlockSpec(memory_space=pl.ANY),
        ],
        out_specs=pl.BlockSpec((bm, width), lambda i, *_: (i, 0)),
        scratch_shapes=[
            pltpu.VMEM((2, d, D_EXPERT), F32),
            pltpu.VMEM((2, d, D_EXPERT), F32),
            pltpu.VMEM((2, D_EXPERT, d), F32),
            pltpu.VMEM((d, D_EXPERT), BF16),
            pltpu.VMEM((d, D_EXPERT), BF16),
            pltpu.VMEM((D_EXPERT, d), BF16),
            pltpu.SemaphoreType.DMA((2,)),
        ],
    )
    return pl.pallas_call(
        _expert_kernel,
        grid_spec=grid_spec,
        out_shape=jax.ShapeDtypeStruct((n_rows, width), U32),
        compiler_params=_params("arbitrary"),
        name="moe_experts",
    )(block_expert, n_used, first, nxt, slot, xs, wg, wu, wd)


def _combine_kernel(d1_ref, d2_ref, d1n_ref, d2n_ref, gate_ref, h1_ref, wf_ref, ys_ref, out_ref,
                    buf, sem, *, tc):
    i = pl.program_id(0)
    slot = i % 2

    def row_copies(da_ref, db_ref, s):
        return [pltpu.make_async_copy(ys_ref.at[pl.ds(d_ref[0, 0, u], 1)],
                                      buf.at[s, k, pl.ds(u, 1)], sem.at[s])
                for u in range(tc) for k, d_ref in enumerate((da_ref, db_ref))]

    @pl.when(i == 0)
    def _():
        for cp in row_copies(d1_ref, d2_ref, 0):
            cp.start()

    @pl.when(i + 1 < pl.num_programs(0))
    def _():
        for cp in row_copies(d1n_ref, d2n_ref, 1 - slot):
            cp.start()

    for cp in row_copies(d1_ref, d2_ref, slot):
        cp.wait()
    gates = gate_ref[...]
    h = h1_ref[...]
    for k in range(TOP_K):
        hi, lo = _unpack_bf16_pair(buf[slot, k])
        y = jnp.concatenate([hi, lo], axis=1).astype(F32)
        h = h + y * gates[:, k:k + 1]
    out_ref[...] = _rms_scale(h) * wf_ref[...]


def _combine_call(d1, d2, gates, h1, wf, ys, tc):
    n_tok, d = h1.shape
    steps = n_tok // tc
    cur = pl.BlockSpec((1, 1, tc), lambda i: (i, 0, 0), memory_space=pltpu.SMEM)
    nxt = pl.BlockSpec((1, 1, tc), lambda i: (jnp.minimum(i + 1, steps - 1), 0, 0),
                       memory_space=pltpu.SMEM)
    d1 = d1.reshape(steps, 1, tc)
    d2 = d2.reshape(steps, 1, tc)
    return pl.pallas_call(
        functools.partial(_combine_kernel, tc=tc),
        grid=(steps,),
        in_specs=[
            cur, cur, nxt, nxt,
            pl.BlockSpec((tc, LANES), lambda i: (i, 0)),
            pl.BlockSpec((tc, d), lambda i: (i, 0)),
            _resident((1, d)),
            pl.BlockSpec(memory_space=pl.ANY),
        ],
        out_specs=pl.BlockSpec((tc, d), lambda i: (i, 0)),
        out_shape=jax.ShapeDtypeStruct((n_tok, d), F32),
        scratch_shapes=[
            pltpu.VMEM((2, TOP_K, tc, d // 2), U32),
            pltpu.SemaphoreType.DMA((2,)),
        ],
        compiler_params=_params("arbitrary"),
        name="moe_combine",
    )(d1, d2, d1, d2, gates, h1, wf, ys)


def _tiles(seq, n_tok):
    return dict(
        proj=min(256, seq),
        gdn=min(512, seq),
        outp=min(512, n_tok),
        route=min(512, n_tok),
        disp=min(256, n_tok),
        bm=256,
        comb=min(256, n_tok),
    )


def _layer(h, n1w, w_in, conv_qkv_w, a_log, dt_bias, gdn_norm_w, conv_sc_w, w_out,
           n2w, wg_r, bg_r, we_r, be_r, w_gate, w_up, w_down, final_w, last):
    bsz, seq, d = h.shape
    n_tok = bsz * seq
    ts = _tiles(seq, n_tok)

    qkvz_cols = 4 * GDN_WIDTH
    w_qkvz = w_in[:, :qkvz_cols].astype(BF16)
    w_ba = jnp.pad(w_in[:, qkvz_cols:qkvz_cols + 2 * GDN_HEADS],
                   ((0, 0), (0, LANES - 2 * GDN_HEADS))).astype(BF16)
    sc0 = qkvz_cols + 2 * GDN_HEADS
    w_sc = w_in[:, sc0:].reshape(d, 3, SC_WIDTH // SC_TILE, SC_TILE)
    w_sc = jnp.swapaxes(w_sc, 1, 2).reshape(d, 3 * SC_WIDTH).astype(BF16)
    pad8 = (GDN_HEADS, LANES - 2 * GDN_HEADS)
    alog_v = jnp.pad(a_log, pad8).reshape(1, LANES)
    dtb_v = jnp.pad(dt_bias, pad8).reshape(1, LANES)

    qkvz, y_sc, bg = _inproj_call(h, n1w.reshape(1, d), jnp.concatenate([w_qkvz, w_sc], axis=1),
                                  w_ba, conv_qkv_w, conv_sc_w, alog_v, dtb_v, ts["proj"])

    gt = jnp.swapaxes(bg[:, :, :2 * GDN_HEADS].reshape(bsz, seq // CHUNK, CHUNK, 2 * GDN_HEADS),
                      2, 3)
    y_gdn = _gdn_call(qkvz, bg, gt, gdn_norm_w.reshape(1, HEAD_DIM), ts["gdn"])

    n_router = N_GROUPS + N_EXPERTS
    wr = jnp.pad(jnp.concatenate([wg_r, we_r], axis=1), ((0, 0), (0, LANES - n_router)))
    wr_hi = wr.astype(BF16)
    wr_t = jnp.concatenate([wr_hi, (wr - wr_hi.astype(F32)).astype(BF16)], axis=1)
    rb = jnp.pad(jnp.concatenate([bg_r, be_r]), (0, LANES - n_router)).reshape(LANES, 1)
    x2d = h.reshape(n_tok, d)
    h1, hp, lt = _outproj_call(y_gdn.reshape(n_tok, GDN_WIDTH), y_sc.reshape(n_tok, SC_WIDTH),
                               x2d, w_out.astype(BF16), n2w.reshape(1, d), wr_t, rb, ts["outp"])

    idx, gates, cnt = _route_call(lt, ts["route"])
    bm = ts["bm"]
    counts = cnt[:, 0].astype(I32)
    padded = (counts + bm - 1) // bm * bm
    pad_end = jnp.cumsum(padded)
    pad_start = pad_end - padded
    d1 = pad_start[idx[0]] + idx[2]
    d2 = pad_start[idx[1]] + idx[3]
    nb = n_tok * TOP_K // bm + N_EXPERTS
    block_row = jnp.arange(nb, dtype=I32) * bm
    block_expert = jnp.minimum(
        jnp.sum((pad_end[None, :] <= block_row[:, None]).astype(I32), axis=1), N_EXPERTS - 1)
    n_used = (pad_end[-1:] // bm).astype(I32)

    has_rows = padded > 0
    e_ids = jnp.arange(N_EXPERTS, dtype=I32)
    later = jnp.where(has_rows, e_ids, N_EXPERTS)
    next_used = jnp.concatenate([lax.cummin(later, reverse=True)[1:],
                                 jnp.full((1,), N_EXPERTS, I32)])
    next_used = jnp.where(next_used < N_EXPERTS, next_used, -1)
    rank_used = jnp.cumsum(has_rows.astype(I32)) - 1
    first = (block_row == pad_start[block_expert]).astype(I32)
    nxt = next_used[block_expert]
    slot = rank_used[block_expert] % 2
    is_last = jnp.any((pad_end[None, :] == (block_row + bm)[:, None]) & has_rows[None, :], axis=1)
    zero_flag = (is_last | (block_row >= pad_end[-1])).astype(I32)

    xs = _dispatch_call(zero_flag, d1, d2, hp, nb * bm, ts["disp"], bm)
    ys = _expert_call(block_expert, n_used, first, nxt, slot, xs, w_gate, w_up, w_down, bm)
    out = _combine_call(d1, d2, gates, h1, final_w.reshape(1, d), ys, ts["comb"])
    del last
    return out.reshape(bsz, seq, d)


def kernel(x, norm1_w, w_in, conv_qkv_w, a_log, dt_bias, gdn_norm_w, conv_sc_w, w_out, norm2_w,
           router_group_w, router_group_b, router_expert_w, router_expert_b, w_gate, w_up,
           w_down, final_norm_w):
    depth = norm1_w.shape[0]
    assert depth == 1, "the combine call applies the final RMSNorm, so a single layer is assumed"
    return _layer(x, norm1_w[0], w_in[0], conv_qkv_w[0], a_log[0], dt_bias[0], gdn_norm_w[0],
                  conv_sc_w[0], w_out[0], norm2_w[0], router_group_w[0], router_group_b[0],
                  router_expert_w[0], router_expert_b[0], w_gate[0], w_up[0], w_down[0],
                  final_norm_w, True)
```

```python
import functools

import jax
import jax.numpy as jnp
from jax import lax
from jax.experimental import pallas as pl
from jax.experimental.pallas import tpu as pltpu

F32 = jnp.float32
BF16 = jnp.bfloat16
I32 = jnp.int32
U32 = jnp.uint32
HIGHEST = lax.Precision.HIGHEST

EPS = 1e-6
D_MODEL = 2048
HEAD_DIM = 128
GDN_HEADS = 8
GDN_WIDTH = GDN_HEADS * HEAD_DIM
GDN_CONV = 4
CHUNK = 64
SC_WIDTH = 1024
SC_CONV = 3
N_GROUPS = 8
EXPERTS_PER_GROUP = 8
N_EXPERTS = N_GROUPS * EXPERTS_PER_GROUP
TOP_K = 2
D_EXPERT = 512
LANES = 128
SUBLANES = 8
SC_TILE = 256
VMEM_LIMIT = 56 * 1024 * 1024


def _sigmoid(x):
    return 1.0 / (1.0 + jnp.exp(-x))


def _silu(x):
    return x * _sigmoid(x)


def _softplus(x):
    return jnp.maximum(x, 0.0) + jnp.log1p(jnp.exp(-jnp.abs(x)))


def _rms_scale(x):
    return x * lax.rsqrt(jnp.mean(x * x, axis=-1, keepdims=True) + EPS)


def _dot(a, b):
    return jnp.dot(a, b, preferred_element_type=F32)


def _hdot(a, b):
    return jnp.dot(a, b, preferred_element_type=F32, precision=HIGHEST)


def _params(*sem):
    return pltpu.CompilerParams(dimension_semantics=sem, vmem_limit_bytes=VMEM_LIMIT)


ROW_PIECE = 64


def _conv_piece(x, prev, cw):
    taps = cw.shape[0]
    rows = x.shape[0]
    ext = jnp.concatenate([prev, x], axis=0)
    y = cw[taps - 1:taps, :] * x
    for s in range(1, taps):
        shifted = pltpu.roll(ext, s, axis=0)[SUBLANES:SUBLANES + rows, :]
        y = y + cw[taps - 1 - s:taps - s, :] * shifted
    return y


def _resident(shape):
    return pl.BlockSpec(shape, lambda *_: (0,) * len(shape), pipeline_mode=pl.Buffered(1))


SC_COL0 = 4 * GDN_WIDTH
PROJ_ORDER = (("qkvz", 0), ("sc", 0), ("qkvz", 1), ("sc", 1), ("qkvz", 2), ("sc", 2),
              ("qkvz", 3), ("sc", 3))


def _inproj_kernel(x_ref, n1w_ref, w_ref, wba_ref, cwq_ref, cws_ref, alog_ref, dtb_ref, ltri_ref,
                   qkvz_ref, ysc_ref, bg_ref, carry_q, carry_s, *, tm):
    @pl.when(pl.program_id(1) == 0)
    def _():
        carry_q[...] = jnp.zeros_like(carry_q)
        carry_s[...] = jnp.zeros_like(carry_s)

    hn = (_rms_scale(x_ref[...]) * n1w_ref[...]).astype(BF16)

    ba = _dot(hn, wba_ref[...])
    lane = lax.broadcasted_iota(I32, ba.shape, 1)
    g = -jnp.exp(alog_ref[...]) * _softplus(ba + dtb_ref[...])
    g = jnp.where((lane >= GDN_HEADS) & (lane < 2 * GDN_HEADS), g, 0.0)
    ltri = ltri_ref[...]
    gc = jnp.concatenate([_hdot(ltri, g[c * CHUNK:(c + 1) * CHUNK, :])
                          for c in range(tm // CHUNK)], axis=0)
    bg_ref[...] = jnp.where(lane < GDN_HEADS, _sigmoid(ba), gc)

    def epilogue(kind, j, p):
        pieces = [(r * ROW_PIECE, c * LANES) for r in range(tm // ROW_PIECE)
                  for c in range((SC_TILE if kind == "sc" else GDN_WIDTH) // LANES)]
        tails = {}
        for r0, c0 in pieces:
            rows = slice(r0, r0 + ROW_PIECE)
            lanes = slice(c0, c0 + LANES)
            if kind == "sc":
                out_cols = slice(SC_TILE * j + c0, SC_TILE * j + c0 + LANES)
                x = p[rows, SC_TILE + c0:SC_TILE + c0 + LANES] * p[
                    rows, 2 * SC_TILE + c0:2 * SC_TILE + c0 + LANES]
                prev = tails.get(c0, carry_s[j, :, lanes]) if r0 else carry_s[j, :, lanes]
                tails[c0] = x[ROW_PIECE - SUBLANES:, :]
                y = _conv_piece(x, prev, cws_ref[:, out_cols])
                ysc_ref[rows, out_cols] = (p[rows, lanes] * y).astype(BF16)
                continue
            out_cols = slice(j * GDN_WIDTH + c0, j * GDN_WIDTH + c0 + LANES)
            x = p[rows, lanes]
            if j == 3:
                qkvz_ref[rows, out_cols] = x.astype(BF16)
                continue
            prev = tails.get(c0, carry_q[j, :, lanes]) if r0 else carry_q[j, :, lanes]
            tails[c0] = x[ROW_PIECE - SUBLANES:, :]
            y = _silu(_conv_piece(x, prev, cwq_ref[:, out_cols]))
            if j < 2:
                scale = HEAD_DIM ** -0.5 if j == 0 else 1.0
                y = y * (lax.rsqrt(jnp.sum(y * y, axis=-1, keepdims=True) + EPS) * scale)
            qkvz_ref[rows, out_cols] = y.astype(BF16)
        carry = carry_s if kind == "sc" else carry_q
        for c0, tail in tails.items():
            carry[j, :, c0:c0 + LANES] = tail

    for kind, j in PROJ_ORDER:
        if kind == "qkvz":
            p = _dot(hn, w_ref[:, j * GDN_WIDTH:(j + 1) * GDN_WIDTH])
        else:
            starts = [SC_COL0 + part * SC_WIDTH + SC_TILE * j for part in range(3)]
            p = jnp.concatenate([_dot(hn, w_ref[:, lo:lo + SC_TILE]) for lo in starts],
                                axis=1)
        epilogue(kind, j, p)


def _inproj_call(x, n1w, w, wba, cwq, cws, alog, dtb, tm):
    bsz, seq, d = x.shape
    idx = jnp.arange(CHUNK)
    ltri = (idx[:, None] >= idx[None, :]).astype(F32)

    def rows(width):
        return pl.BlockSpec((None, tm, width), lambda b, i: (b, i, 0))

    return pl.pallas_call(
        functools.partial(_inproj_kernel, tm=tm),
        grid=(bsz, seq // tm),
        in_specs=[
            rows(d),
            _resident((1, d)),
            _resident((d, SC_COL0 + 3 * SC_WIDTH)),
            _resident((d, LANES)),
            _resident((GDN_CONV, 3 * GDN_WIDTH)),
            _resident((SC_CONV, SC_WIDTH)),
            _resident((1, LANES)),
            _resident((1, LANES)),
            _resident((CHUNK, CHUNK)),
        ],
        out_specs=[rows(4 * GDN_WIDTH), rows(SC_WIDTH), rows(LANES)],
        out_shape=[
            jax.ShapeDtypeStruct((bsz, seq, 4 * GDN_WIDTH), BF16),
            jax.ShapeDtypeStruct((bsz, seq, SC_WIDTH), BF16),
            jax.ShapeDtypeStruct((bsz, seq, LANES), F32),
        ],
        scratch_shapes=[
            pltpu.VMEM((3, SUBLANES, GDN_WIDTH), F32),
            pltpu.VMEM((SC_WIDTH // SC_TILE, SUBLANES, SC_TILE), F32),
        ],
        compiler_params=_params("arbitrary", "arbitrary"),
        name="in_proj",
    )(x, n1w, w, wba, cwq, cws, alog, dtb, ltri)


GDN_CHUNKS_PER_TRIP = 4


def _sdot(a, b):
    return _dot(a.astype(BF16), b.astype(BF16))


def _gdn_kernel(q_ref, k_ref, v_ref, z_ref, bg_ref, gt_ref, nw_ref, out_ref, state_sc, *, tc):
    @pl.when(pl.program_id(1) == 0)
    def _():
        state_sc[...] = jnp.zeros_like(state_sc)

    row = lax.broadcasted_iota(I32, (CHUNK, CHUNK), 0)
    col = lax.broadcasted_iota(I32, (CHUNK, CHUNK), 1)
    causal = row >= col
    strict = row > col
    eye = (row == col).astype(F32)
    same16 = (row // 16) == (col // 16)
    same32 = (row // 32) == (col // 32)
    m16 = same16.astype(F32)
    m32 = (same32 & jnp.logical_not(same16)).astype(F32)
    m64 = jnp.logical_not(same32).astype(F32)
    nw = nw_ref[...]
    tt = (((0,), (0,)), ((), ()))
    nt = (((1,), (1,)), ((), ()))
    heads = range(GDN_HEADS)

    def hcols(h):
        return slice(h * HEAD_DIM, (h + 1) * HEAD_DIM)

    def trip_body(t, carry):
        units = []
        for cc in range(GDN_CHUNKS_PER_TRIP):
            c = t * GDN_CHUNKS_PER_TRIP + cc
            rows = pl.ds(pl.multiple_of(c * CHUNK, CHUNK), CHUNK)
            bgc = bg_ref[rows, :]
            gt = gt_ref[c]
            for h in heads:
                gc = bgc[:, GDN_HEADS + h:GDN_HEADS + h + 1]
                units.append(dict(
                    rows=rows, h=h, gc=gc,
                    beta=bgc[:, h:h + 1],
                    gcr=gt[GDN_HEADS + h:GDN_HEADS + h + 1, :],
                    glast=gc[CHUNK - 1:CHUNK, :],
                    q=q_ref[rows, hcols(h)], k=k_ref[rows, hcols(h)]))

        for u in units:
            u["kq"] = lax.dot_general(jnp.concatenate([u["k"], u["q"]], axis=0), u["k"], nt,
                                      preferred_element_type=F32)
        for u in units:
            decay = jnp.where(causal, jnp.exp(u["gc"] - u["gcr"]), 0.0)
            u["a"] = jnp.where(strict, u["kq"][:CHUNK] * u["beta"] * decay, 0.0)
            u["attn"] = (u["kq"][CHUNK:] * decay).astype(BF16)

        for u in units:
            u["n"] = -(u["a"] * m16)
            u["p"] = eye + u["n"]
        for _ in range(3):
            for u in units:
                u["n"] = _sdot(u["n"], u["n"])
            for u in units:
                u["p"] = u["p"] + _sdot(u["p"], u["n"])
        for m in (m32, m64):
            for u in units:
                u["e"] = _sdot(u["a"] * m, u["p"])
            for u in units:
                u["p"] = u["p"] - _sdot(u["p"], u["e"])

        for u in units:
            u["eg"] = jnp.exp(u["gc"])
            u["kf"] = u["k"].astype(F32)
            v = v_ref[u["rows"], hcols(u["h"])].astype(F32)
            bw = u["beta"] * u["eg"]
            rhs = jnp.concatenate([v * u["beta"], u["kf"] * bw], axis=1)
            u["uw"] = _sdot(u["p"], rhs)

        state = [state_sc[h] for h in heads]
        for cc in range(GDN_CHUNKS_PER_TRIP):
            us = units[cc * GDN_HEADS:(cc + 1) * GDN_HEADS]
            for u in us:
                wq = jnp.concatenate([u["uw"][:, HEAD_DIM:], u["q"].astype(F32) * u["eg"]], axis=0)
                u["rs"] = _dot(wq.astype(BF16), state[u["h"]].astype(BF16))
            for u in us:
                u["vb"] = (u["uw"][:, :HEAD_DIM] - u["rs"][:CHUNK]).astype(BF16)
                u["o"] = u["rs"][CHUNK:] + _dot(u["attn"], u["vb"])
            for u in us:
                k_dec = (u["kf"] * jnp.exp(u["glast"] - u["gc"])).astype(BF16)
                state[u["h"]] = state[u["h"]] * jnp.exp(u["glast"]) + lax.dot_general(
                    k_dec, u["vb"], tt, preferred_element_type=F32)
            for u in us:
                z = z_ref[u["rows"], hcols(u["h"])].astype(F32)
                out_ref[u["rows"], hcols(u["h"])] = (
                    _rms_scale(u["o"]) * nw * _silu(z)).astype(BF16)
        for h in heads:
            state_sc[h] = state[h]
        return carry

    lax.fori_loop(0, tc // (CHUNK * GDN_CHUNKS_PER_TRIP), trip_body, 0)


def _gdn_call(qkvz, bg, gt, nw, tc):
    bsz, seq, _ = qkvz.shape

    def col(j):
        return pl.BlockSpec((None, tc, GDN_WIDTH), lambda b, i: (b, i, j))

    return pl.pallas_call(
        functools.partial(_gdn_kernel, tc=tc),
        grid=(bsz, seq // tc),
        in_specs=[
            col(0), col(1), col(2), col(3),
            pl.BlockSpec((None, tc, LANES), lambda b, i: (b, i, 0)),
            pl.BlockSpec((None, tc // CHUNK, 2 * GDN_HEADS, CHUNK), lambda b, i: (b, i, 0, 0)),
            _resident((1, HEAD_DIM)),
        ],
        out_specs=pl.BlockSpec((None, tc, GDN_WIDTH), lambda b, i: (b, i, 0)),
        out_shape=jax.ShapeDtypeStruct((bsz, seq, GDN_WIDTH), BF16),
        scratch_shapes=[pltpu.VMEM((GDN_HEADS, HEAD_DIM, HEAD_DIM), F32)],
        compiler_params=_params("arbitrary", "arbitrary"),
        name="gated_delta_rule",
    )(qkvz, qkvz, qkvz, qkvz, bg, gt, nw)


def _pack_bf16_pair(hi, lo):
    hb = lax.bitcast_convert_type(hi.astype(BF16).astype(F32), U32)
    lb = lax.bitcast_convert_type(lo.astype(BF16).astype(F32), U32)
    return (hb & jnp.uint32(0xFFFF0000)) | lax.shift_right_logical(lb, jnp.uint32(16))


def _unpack_bf16_pair(p):
    hi = lax.bitcast_convert_type(p & jnp.uint32(0xFFFF0000), F32).astype(BF16)
    lo = lax.bitcast_convert_type(lax.shift_left(p, jnp.uint32(16)), F32).astype(BF16)
    return hi, lo


def _outproj_kernel(yg_ref, ys_ref, x_ref, wo_ref, n2w_ref, wr_ref, rb_ref,
                    h1_ref, hp_ref, lt_ref):
    h1 = (x_ref[...] + _dot(yg_ref[...], wo_ref[0:GDN_WIDTH, :])
          + _dot(ys_ref[...], wo_ref[GDN_WIDTH:GDN_WIDTH + SC_WIDTH, :]))
    h1_ref[...] = h1
    hn = _rms_scale(h1) * n2w_ref[...]
    hn_hi = hn.astype(BF16)
    hn_lo = (hn - hn_hi.astype(F32)).astype(BF16)
    lg = _dot(hn_hi, wr_ref[...])
    lg = lg[:, :LANES] + lg[:, LANES:] + _dot(hn_lo, wr_ref[:, :LANES])
    lt_ref[...] = lg.T + rb_ref[...]
    half = D_MODEL // 2
    hp_ref[...] = _pack_bf16_pair(hn[:, :half], hn[:, half:])


def _outproj_call(yg, ys, x2d, wo, n2w, wr_t, rb, tm):
    n_tok, d = x2d.shape
    return pl.pallas_call(
        _outproj_kernel,
        grid=(n_tok // tm,),
        in_specs=[
            pl.BlockSpec((tm, GDN_WIDTH), lambda i: (i, 0)),
            pl.BlockSpec((tm, SC_WIDTH), lambda i: (i, 0)),
            pl.BlockSpec((tm, d), lambda i: (i, 0)),
            _resident((GDN_WIDTH + SC_WIDTH, d)),
            _resident((1, d)),
            _resident((d, 2 * LANES)),
            _resident((LANES, 1)),
        ],
        out_specs=[
            pl.BlockSpec((tm, d), lambda i: (i, 0)),
            pl.BlockSpec((tm, d // 2), lambda i: (i, 0)),
            pl.BlockSpec((LANES, tm), lambda i: (0, i)),
        ],
        out_shape=[
            jax.ShapeDtypeStruct((n_tok, d), F32),
            jax.ShapeDtypeStruct((n_tok, d // 2), U32),
            jax.ShapeDtypeStruct((LANES, n_tok), F32),
        ],
        compiler_params=_params("arbitrary"),
        name="out_proj_router",
    )(yg, ys, x2d, wo, n2w, wr_t, rb)


def _first_argmax(vals, iota, size):
    m = jnp.max(vals, axis=0, keepdims=True)
    idx = jnp.min(jnp.where(vals == m, iota, size), axis=0, keepdims=True)
    return m, idx


def _route_kernel(lt_ref, idx_ref, gate_ref, cnt_ref, carry_sc, *, tr):
    @pl.when(pl.program_id(0) == 0)
    def _():
        carry_sc[...] = jnp.zeros_like(carry_sc)

    iota8 = lax.broadcasted_iota(I32, (EXPERTS_PER_GROUP, tr), 0)
    gl = lt_ref[0:N_GROUPS, :]
    gmax, gidx = _first_argmax(gl, iota8, N_GROUPS)
    g_w = 1.0 / jnp.sum(jnp.exp(gl - gmax), axis=0, keepdims=True)

    ig = jnp.zeros((EXPERTS_PER_GROUP, tr), F32)
    for g in range(N_GROUPS):
        lo = N_GROUPS + g * EXPERTS_PER_GROUP
        ig = jnp.where(gidx == g, lt_ref[lo:lo + EXPERTS_PER_GROUP, :], ig)
    pe = jnp.exp(ig - jnp.max(ig, axis=0, keepdims=True))
    p = pe / jnp.sum(pe, axis=0, keepdims=True)
    p1, i1 = _first_argmax(p, iota8, EXPERTS_PER_GROUP)
    p2, i2 = _first_argmax(jnp.where(iota8 == i1, -1.0, p), iota8, EXPERTS_PER_GROUP)
    denom = p1 + p2
    e1 = gidx * EXPERTS_PER_GROUP + i1
    e2 = gidx * EXPERTS_PER_GROUP + i2

    iota_e = lax.broadcasted_iota(I32, (N_EXPERTS, tr), 0)
    oh1 = (iota_e == e1).astype(F32)
    oh2 = (iota_e == e2).astype(F32)
    ohs = oh1 + oh2
    r = lax.broadcasted_iota(I32, (tr, tr), 0)
    c = lax.broadcasted_iota(I32, (tr, tr), 1)
    before = jnp.where(r < c, 1.0, 0.0).astype(BF16)
    base = carry_sc[:, 0:1] + _dot(ohs.astype(BF16), before)
    rank1 = jnp.sum(oh1 * base, axis=0, keepdims=True)
    rank2 = jnp.sum(oh2 * base, axis=0, keepdims=True)
    carry_sc[...] = carry_sc[...] + jnp.sum(ohs, axis=1, keepdims=True)
    cnt_ref[...] = carry_sc[...]

    idx_ref[...] = jnp.zeros_like(idx_ref)
    idx_ref[0:1, :] = e1
    idx_ref[1:2, :] = e2
    idx_ref[2:3, :] = rank1.astype(I32)
    idx_ref[3:4, :] = rank2.astype(I32)
    sub = lax.broadcasted_iota(I32, (LANES, tr), 0)
    gmat = jnp.where(sub == 0, g_w * p1 / denom, jnp.where(sub == 1, g_w * p2 / denom, 0.0))
    gate_ref[...] = gmat.T


def _route_call(lt, tr):
    n_tok = lt.shape[1]
    return pl.pallas_call(
        functools.partial(_route_kernel, tr=tr),
        grid=(n_tok // tr,),
        in_specs=[pl.BlockSpec((LANES, tr), lambda i: (0, i))],
        out_specs=[
            pl.BlockSpec((SUBLANES, tr), lambda i: (0, i)),
            pl.BlockSpec((tr, LANES), lambda i: (i, 0)),
            pl.BlockSpec((N_EXPERTS, LANES), lambda i: (0, 0)),
        ],
        out_shape=[
            jax.ShapeDtypeStruct((SUBLANES, n_tok), I32),
            jax.ShapeDtypeStruct((n_tok, LANES), F32),
            jax.ShapeDtypeStruct((N_EXPERTS, LANES), F32),
        ],
        scratch_shapes=[pltpu.VMEM((N_EXPERTS, LANES), F32)],
        compiler_params=_params("arbitrary"),
        name="moe_route",
    )(lt)


DISPATCH_SLOTS = 3


def _dispatch_kernel(zf_ref, d1_ref, d2_ref, d1p_ref, d2p_ref, hp_ref, xs_ref,
                     stage, zbuf, sem_in, sem_out, sem_z, *, tb, bm):
    i = pl.program_id(0)
    last = pl.num_programs(0) - 1

    def block_load(step):
        s = step % DISPATCH_SLOTS
        return pltpu.make_async_copy(hp_ref.at[pl.ds(step * tb, tb)], stage.at[s], sem_in.at[s])

    def row_copies(step, da_ref, db_ref):
        s = step % DISPATCH_SLOTS
        return [pltpu.make_async_copy(stage.at[s, pl.ds(u, 1)],
                                      xs_ref.at[pl.ds(d_ref[0, 0, u], 1)], sem_out.at[s])
                for u in range(tb) for d_ref in (da_ref, db_ref)]

    def zero_fill(b):
        return pltpu.make_async_copy(zbuf, xs_ref.at[pl.ds(pl.multiple_of(b * bm, bm), bm)], sem_z)

    @pl.when(i == 0)
    def _():
        block_load(i).start()
        zbuf[...] = jnp.zeros_like(zbuf)

        def start(b, carry):
            @pl.when(zf_ref[b] != 0)
            def _():
                zero_fill(b).start()
            return carry

        def wait(b, carry):
            @pl.when(zf_ref[b] != 0)
            def _():
                zero_fill(b).wait()
            return carry

        lax.fori_loop(0, zf_ref.shape[0], start, 0)
        lax.fori_loop(0, zf_ref.shape[0], wait, 0)

    @pl.when(i < last)
    def _():
        block_load(i + 1).start()

    block_load(i).wait()
    for cp in row_copies(i, d1_ref, d2_ref):
        cp.start()

    @pl.when(i > 0)
    def _():
        for cp in row_copies(i - 1, d1p_ref, d2p_ref):
            cp.wait()

    @pl.when(i == last)
    def _():
        for cp in row_copies(i, d1_ref, d2_ref):
            cp.wait()


def _dispatch_call(zero_flag, d1, d2, hp, n_rows, tb, bm):
    n_tok, width = hp.shape
    cur = pl.BlockSpec((1, 1, tb), lambda i, zf: (i, 0, 0), memory_space=pltpu.SMEM)
    prev = pl.BlockSpec((1, 1, tb), lambda i, zf: (jnp.maximum(i - 1, 0), 0, 0),
                        memory_space=pltpu.SMEM)
    d1 = d1.reshape(n_tok // tb, 1, tb)
    d2 = d2.reshape(n_tok // tb, 1, tb)
    grid_spec = pltpu.PrefetchScalarGridSpec(
        num_scalar_prefetch=1,
        grid=(n_tok // tb,),
        in_specs=[cur, cur, prev, prev, pl.BlockSpec(memory_space=pl.ANY)],
        out_specs=pl.BlockSpec(memory_space=pl.ANY),
        scratch_shapes=[
            pltpu.VMEM((DISPATCH_SLOTS, tb, width), U32),
            pltpu.VMEM((bm, width), U32),
            pltpu.SemaphoreType.DMA((DISPATCH_SLOTS,)),
            pltpu.SemaphoreType.DMA((DISPATCH_SLOTS,)),
            pltpu.SemaphoreType.DMA(()),
        ],
    )
    return pl.pallas_call(
        functools.partial(_dispatch_kernel, tb=tb, bm=bm),
        grid_spec=grid_spec,
        out_shape=jax.ShapeDtypeStruct((n_rows, width), U32),
        compiler_params=_params("arbitrary"),
        name="moe_dispatch",
    )(zero_flag, d1, d2, d1, d2, hp)


def _expert_kernel(be_ref, nu_ref, first_ref, nxt_ref, slot_ref, xs_ref, wg_hbm, wu_hbm, wd_hbm,
                   ys_ref, wg32, wu32, wd32, wgb, wub, wdb, sem):
    i = pl.program_id(0)
    used = i < nu_ref[0]

    def weight_copies(e, s):
        return [pltpu.make_async_copy(src.at[e], dst.at[s], sem.at[s])
                for src, dst in ((wg_hbm, wg32), (wu_hbm, wu32), (wd_hbm, wd32))]

    @pl.when(used & (first_ref[i] != 0))
    def _():
        e = be_ref[i]
        s = slot_ref[i]

        @pl.when(i == 0)
        def _():
            for cp in weight_copies(e, s):
                cp.start()

        for cp in weight_copies(e, s):
            cp.wait()

        @pl.when(nxt_ref[i] >= 0)
        def _():
            for cp in weight_copies(nxt_ref[i], 1 - s):
                cp.start()

        wgb[...] = wg32[s].astype(BF16)
        wub[...] = wu32[s].astype(BF16)
        wdb[...] = wd32[s].astype(BF16)

    @pl.when(used)
    def _():
        hi, lo = _unpack_bf16_pair(xs_ref[...])
        xb = jnp.concatenate([hi, lo], axis=1)
        hid = _silu(_dot(xb, wgb[...])) * _dot(xb, wub[...])
        y = _dot(hid.astype(BF16), wdb[...])
        ys_ref[...] = _pack_bf16_pair(y[:, :D_MODEL // 2], y[:, D_MODEL // 2:])

    @pl.when(jnp.logical_not(used))
    def _():
        ys_ref[...] = jnp.zeros_like(ys_ref)


def _expert_call(block_expert, n_used, first, nxt, slot, xs, wg, wu, wd, bm):
    n_rows, width = xs.shape
    d = 2 * width
    nb = n_rows // bm
    grid_spec = pltpu.PrefetchScalarGridSpec(
        num_scalar_prefetch=5,
        grid=(nb,),
        in_specs=[
            pl.BlockSpec((bm, width), lambda i, be, nu, *_: (jnp.minimum(i, nu[0] - 1), 0)),
            pl.BlockSpec(memory_space=pl.ANY),
            pl.BlockSpec(memory_space=pl.ANY),
            pl.BlockSpec(memory_space=pl.ANY),
        ],
        out_specs=pl.BlockSpec((bm, width), lambda i, *_: (i, 0)),
        scratch_shapes=[
            pltpu.VMEM((2, d, D_EXPERT), F32),
            pltpu.VMEM((2, d, D_EXPERT), F32),
            pltpu.VMEM((2, D_EXPERT, d), F32),
            pltpu.VMEM((d, D_EXPERT), BF16),
            pltpu.VMEM((d, D_EXPERT), BF16),
            pltpu.VMEM((D_EXPERT, d), BF16),
            pltpu.SemaphoreType.DMA((2,)),
        ],
    )
    return pl.pallas_call(
        _expert_kernel,
        grid_spec=grid_spec,
        out_shape=jax.ShapeDtypeStruct((n_rows, width), U32),
        compiler_params=_params("arbitrary"),
        name="moe_experts",
    )(block_expert, n_used, first, nxt, slot, xs, wg, wu, wd)


def _combine_kernel(d1_ref, d2_ref, d1n_ref, d2n_ref, gate_ref, h1_ref, wf_ref, ys_ref, out_ref,
                    buf, sem, *, tc):
    i = pl.program_id(0)
    slot = i % 2

    def row_copies(da_ref, db_ref, s):
        return [pltpu.make_async_copy(ys_ref.at[pl.ds(d_ref[0, 0, u], 1)],
                                      buf.at[s, k, pl.ds(u, 1)], sem.at[s])
                for u in range(tc) for k, d_ref in enumerate((da_ref, db_ref))]

    @pl.when(i == 0)
    def _():
        for cp in row_copies(d1_ref, d2_ref, 0):
            cp.start()

    @pl.when(i + 1 < pl.num_programs(0))
    def _():
        for cp in row_copies(d1n_ref, d2n_ref, 1 - slot):
            cp.start()

    for cp in row_copies(d1_ref, d2_ref, slot):
        cp.wait()
    gates = gate_ref[...]
    h = h1_ref[...]
    for k in range(TOP_K):
        hi, lo = _unpack_bf16_pair(buf[slot, k])
        y = jnp.concatenate([hi, lo], axis=1).astype(F32)
        h = h + y * gates[:, k:k + 1]
    out_ref[...] = _rms_scale(h) * wf_ref[...]


def _combine_call(d1, d2, gates, h1, wf, ys, tc):
    n_tok, d = h1.shape
    steps = n_tok // tc
    cur = pl.BlockSpec((1, 1, tc), lambda i: (i, 0, 0), memory_space=pltpu.SMEM)
    nxt = pl.BlockSpec((1, 1, tc), lambda i: (jnp.minimum(i + 1, steps - 1), 0, 0),
                       memory_space=pltpu.SMEM)
    d1 = d1.reshape(steps, 1, tc)
    d2 = d2.reshape(steps, 1, tc)
    return pl.pallas_call(
        functools.partial(_combine_kernel, tc=tc),
        grid=(steps,),
        in_specs=[
            cur, cur, nxt, nxt,
            pl.BlockSpec((tc, LANES), lambda i: (i, 0)),
            pl.BlockSpec((tc, d), lambda i: (i, 0)),
            _resident((1, d)),
            pl.BlockSpec(memory_space=pl.ANY),
        ],
        out_specs=pl.BlockSpec((tc, d), lambda i: (i, 0)),
        out_shape=jax.ShapeDtypeStruct((n_tok, d), F32),
        scratch_shapes=[
            pltpu.VMEM((2, TOP_K, tc, d // 2), U32),
            pltpu.SemaphoreType.DMA((2,)),
        ],
        compiler_params=_params("arbitrary"),
        name="moe_combine",
    )(d1, d2, d1, d2, gates, h1, wf, ys)


def _tiles(seq, n_tok):
    return dict(
        proj=min(256, seq),
        gdn=min(512, seq),
        outp=min(512, n_tok),
        route=min(512, n_tok),
        disp=min(256, n_tok),
        bm=256,
        comb=min(256, n_tok),
    )


def _table_lookup(table, keys):
    e_ids = jnp.arange(table.shape[0], dtype=I32)
    return jnp.sum(jnp.where(e_ids[:, None] == keys[None, :], table[:, None], 0), axis=0)


def _layer(h, n1w, w_in, conv_qkv_w, a_log, dt_bias, gdn_norm_w, conv_sc_w, w_out,
           n2w, wg_r, bg_r, we_r, be_r, w_gate, w_up, w_down, final_w, last):
    bsz, seq, d = h.shape
    n_tok = bsz * seq
    ts = _tiles(seq, n_tok)

    qkvz_cols = 4 * GDN_WIDTH
    sc0 = qkvz_cols + 2 * GDN_HEADS
    w_main = jnp.concatenate([w_in[:, :qkvz_cols], w_in[:, sc0:]], axis=1).astype(BF16)
    w_ba = jnp.pad(w_in[:, qkvz_cols:sc0], ((0, 0), (0, LANES - 2 * GDN_HEADS))).astype(BF16)
    pad8 = (GDN_HEADS, LANES - 2 * GDN_HEADS)
    alog_v = jnp.pad(a_log, pad8).reshape(1, LANES)
    dtb_v = jnp.pad(dt_bias, pad8).reshape(1, LANES)

    qkvz, y_sc, bg = _inproj_call(h, n1w.reshape(1, d), w_main, w_ba, conv_qkv_w, conv_sc_w,
                                  alog_v, dtb_v, ts["proj"])

    gt = jnp.swapaxes(bg[:, :, :2 * GDN_HEADS].reshape(bsz, seq // CHUNK, CHUNK, 2 * GDN_HEADS),
                      2, 3)
    y_gdn = _gdn_call(qkvz, bg, gt, gdn_norm_w.reshape(1, HEAD_DIM), ts["gdn"])

    n_router = N_GROUPS + N_EXPERTS
    wr = jnp.pad(jnp.concatenate([wg_r, we_r], axis=1), ((0, 0), (0, LANES - n_router)))
    wr_hi = wr.astype(BF16)
    wr_t = jnp.concatenate([wr_hi, (wr - wr_hi.astype(F32)).astype(BF16)], axis=1)
    rb = jnp.pad(jnp.concatenate([bg_r, be_r]), (0, LANES - n_router)).reshape(LANES, 1)
    x2d = h.reshape(n_tok, d)
    h1, hp, lt = _outproj_call(y_gdn.reshape(n_tok, GDN_WIDTH), y_sc.reshape(n_tok, SC_WIDTH),
                               x2d, w_out.astype(BF16), n2w.reshape(1, d), wr_t, rb, ts["outp"])

    idx, gates, cnt = _route_call(lt, ts["route"])
    bm = ts["bm"]
    counts = cnt[:, 0].astype(I32)
    padded = (counts + bm - 1) // bm * bm
    pad_end = jnp.cumsum(padded)
    pad_start = pad_end - padded
    d1 = _table_lookup(pad_start, idx[0]) + idx[2]
    d2 = _table_lookup(pad_start, idx[1]) + idx[3]
    nb = n_tok * TOP_K // bm + N_EXPERTS
    block_row = jnp.arange(nb, dtype=I32) * bm
    block_expert = jnp.minimum(
        jnp.sum((pad_end[None, :] <= block_row[:, None]).astype(I32), axis=1), N_EXPERTS - 1)
    n_used = (pad_end[-1:] // bm).astype(I32)

    has_rows = padded > 0
    e_ids = jnp.arange(N_EXPERTS, dtype=I32)
    later = jnp.where(has_rows, e_ids, N_EXPERTS)
    next_used = jnp.concatenate([lax.cummin(later, reverse=True)[1:],
                                 jnp.full((1,), N_EXPERTS, I32)])
    next_used = jnp.where(next_used < N_EXPERTS, next_used, -1)
    rank_used = jnp.cumsum(has_rows.astype(I32)) - 1
    first = (block_row == _table_lookup(pad_start, block_expert)).astype(I32)
    nxt = _table_lookup(next_used, block_expert)
    slot = _table_lookup(rank_used, block_expert) % 2
    is_last = jnp.any((pad_end[None, :] == (block_row + bm)[:, None]) & has_rows[None, :], axis=1)
    zero_flag = (is_last | (block_row >= pad_end[-1])).astype(I32)

    xs = _dispatch_call(zero_flag, d1, d2, hp, nb * bm, ts["disp"], bm)
    ys = _expert_call(block_expert, n_used, first, nxt, slot, xs, w_gate, w_up, w_down, bm)
    out = _combine_call(d1, d2, gates, h1, final_w.reshape(1, d), ys, ts["comb"])
    del last
    return out.reshape(bsz, seq, d)


def kernel(x, norm1_w, w_in, conv_qkv_w, a_log, dt_bias, gdn_norm_w, conv_sc_w, w_out, norm2_w,
           router_group_w, router_group_b, router_expert_w, router_expert_b, w_gate, w_up,
           w_down, final_norm_w):
    depth = norm1_w.shape[0]
    assert depth == 1, "the combine call applies the final RMSNorm, so a single layer is assumed"
    return _layer(x, norm1_w[0], w_in[0], conv_qkv_w[0], a_log[0], dt_bias[0], gdn_norm_w[0],
                  conv_sc_w[0], w_out[0], norm2_w[0], router_group_w[0], router_group_b[0],
                  router_expert_w[0], router_expert_b[0], w_gate[0], w_up[0], w_down[0],
                  final_norm_w, True)
```

```python
import functools

import jax
import jax.numpy as jnp
from jax import lax
from jax.experimental import pallas as pl
from jax.experimental.pallas import tpu as pltpu

F32 = jnp.float32
BF16 = jnp.bfloat16
I32 = jnp.int32
U32 = jnp.uint32
HIGHEST = lax.Precision.HIGHEST

EPS = 1e-6
D_MODEL = 2048
HEAD_DIM = 128
GDN_HEADS = 8
GDN_WIDTH = GDN_HEADS * HEAD_DIM
GDN_CONV = 4
CHUNK = 64
SC_WIDTH = 1024
SC_CONV = 3
N_GROUPS = 8
EXPERTS_PER_GROUP = 8
N_EXPERTS = N_GROUPS * EXPERTS_PER_GROUP
TOP_K = 2
D_EXPERT = 512
LANES = 128
SUBLANES = 8
SC_TILE = 256
VMEM_LIMIT = 56 * 1024 * 1024


def _sigmoid(x):
    return 1.0 / (1.0 + jnp.exp(-x))


def _silu(x):
    half = 0.5 * x
    return half + half * jnp.tanh(half)


def _softplus(x):
    return jnp.maximum(x, 0.0) + jnp.log1p(jnp.exp(-jnp.abs(x)))


def _rms_scale(x):
    return x * lax.rsqrt(jnp.mean(x * x, axis=-1, keepdims=True) + EPS)


def _dot(a, b):
    return jnp.dot(a, b, preferred_element_type=F32)


def _hdot(a, b):
    return jnp.dot(a, b, preferred_element_type=F32, precision=HIGHEST)


def _params(*sem):
    return pltpu.CompilerParams(dimension_semantics=sem, vmem_limit_bytes=VMEM_LIMIT)


ROW_PIECE = 64


def _conv_piece(x, prev, cw):
    taps = cw.shape[0]
    rows = x.shape[0]
    ext = jnp.concatenate([prev, x], axis=0)
    y = cw[taps - 1:taps, :] * x
    for s in range(1, taps):
        shifted = pltpu.roll(ext, s, axis=0)[SUBLANES:SUBLANES + rows, :]
        y = y + cw[taps - 1 - s:taps - s, :] * shifted
    return y


def _resident(shape):
    return pl.BlockSpec(shape, lambda *_: (0,) * len(shape), pipeline_mode=pl.Buffered(1))


SC_COL0 = 4 * GDN_WIDTH
PROJ_ORDER = (("qkvz", 0), ("sc", 0), ("qkvz", 1), ("sc", 1), ("qkvz", 2), ("sc", 2),
              ("qkvz", 3), ("sc", 3))


def _inproj_kernel(x_ref, n1w_ref, w_ref, wba_ref, cwq_ref, cws_ref, alog_ref, dtb_ref, ltri_ref,
                   qkvz_ref, ysc_ref, bg_ref, carry_q, carry_s, *, tm):
    @pl.when(pl.program_id(1) == 0)
    def _():
        carry_q[...] = jnp.zeros_like(carry_q)
        carry_s[...] = jnp.zeros_like(carry_s)

    hn = (_rms_scale(x_ref[...]) * n1w_ref[...]).astype(BF16)

    ba = _dot(hn, wba_ref[...])
    lane = lax.broadcasted_iota(I32, ba.shape, 1)
    g = -jnp.exp(alog_ref[...]) * _softplus(ba + dtb_ref[...])
    g = jnp.where((lane >= GDN_HEADS) & (lane < 2 * GDN_HEADS), g, 0.0)
    ltri = ltri_ref[...]
    gc = jnp.concatenate([_hdot(ltri, g[c * CHUNK:(c + 1) * CHUNK, :])
                          for c in range(tm // CHUNK)], axis=0)
    bg_ref[...] = jnp.where(lane < GDN_HEADS, _sigmoid(ba), gc)

    def epilogue(kind, j, p):
        pieces = [(r * ROW_PIECE, c * LANES) for r in range(tm // ROW_PIECE)
                  for c in range((SC_TILE if kind == "sc" else GDN_WIDTH) // LANES)]
        tails = {}
        for r0, c0 in pieces:
            rows = slice(r0, r0 + ROW_PIECE)
            lanes = slice(c0, c0 + LANES)
            if kind == "sc":
                out_cols = slice(SC_TILE * j + c0, SC_TILE * j + c0 + LANES)
                x = p[rows, SC_TILE + c0:SC_TILE + c0 + LANES] * p[
                    rows, 2 * SC_TILE + c0:2 * SC_TILE + c0 + LANES]
                prev = tails.get(c0, carry_s[j, :, lanes]) if r0 else carry_s[j, :, lanes]
                tails[c0] = x[ROW_PIECE - SUBLANES:, :]
                y = _conv_piece(x, prev, cws_ref[:, out_cols])
                ysc_ref[rows, out_cols] = (p[rows, lanes] * y).astype(BF16)
                continue
            out_cols = slice(j * GDN_WIDTH + c0, j * GDN_WIDTH + c0 + LANES)
            x = p[rows, lanes]
            if j == 3:
                qkvz_ref[rows, out_cols] = x.astype(BF16)
                continue
            prev = tails.get(c0, carry_q[j, :, lanes]) if r0 else carry_q[j, :, lanes]
            tails[c0] = x[ROW_PIECE - SUBLANES:, :]
            y = _silu(_conv_piece(x, prev, cwq_ref[:, out_cols]))
            if j < 2:
                scale = HEAD_DIM ** -0.5 if j == 0 else 1.0
                y = y * (lax.rsqrt(jnp.sum(y * y, axis=-1, keepdims=True) + EPS) * scale)
            qkvz_ref[rows, out_cols] = y.astype(BF16)
        carry = carry_s if kind == "sc" else carry_q
        for c0, tail in tails.items():
            carry[j, :, c0:c0 + LANES] = tail

    for kind, j in PROJ_ORDER:
        if kind == "qkvz":
            p = _dot(hn, w_ref[:, j * GDN_WIDTH:(j + 1) * GDN_WIDTH])
        else:
            starts = [SC_COL0 + part * SC_WIDTH + SC_TILE * j for part in range(3)]
            p = jnp.concatenate([_dot(hn, w_ref[:, lo:lo + SC_TILE]) for lo in starts],
                                axis=1)
        epilogue(kind, j, p)


def _inproj_call(x, n1w, w, wba, cwq, cws, alog, dtb, tm):
    bsz, seq, d = x.shape
    idx = jnp.arange(CHUNK)
    ltri = (idx[:, None] >= idx[None, :]).astype(F32)

    def rows(width):
        return pl.BlockSpec((None, tm, width), lambda b, i: (b, i, 0))

    return pl.pallas_call(
        functools.partial(_inproj_kernel, tm=tm),
        grid=(bsz, seq // tm),
        in_specs=[
            rows(d),
            _resident((1, d)),
            _resident((d, SC_COL0 + 3 * SC_WIDTH)),
            _resident((d, LANES)),
            _resident((GDN_CONV, 3 * GDN_WIDTH)),
            _resident((SC_CONV, SC_WIDTH)),
            _resident((1, LANES)),
            _resident((1, LANES)),
            _resident((CHUNK, CHUNK)),
        ],
        out_specs=[rows(4 * GDN_WIDTH), rows(SC_WIDTH), rows(LANES)],
        out_shape=[
            jax.ShapeDtypeStruct((bsz, seq, 4 * GDN_WIDTH), BF16),
            jax.ShapeDtypeStruct((bsz, seq, SC_WIDTH), BF16),
            jax.ShapeDtypeStruct((bsz, seq, LANES), F32),
        ],
        scratch_shapes=[
            pltpu.VMEM((3, SUBLANES, GDN_WIDTH), F32),
            pltpu.VMEM((SC_WIDTH // SC_TILE, SUBLANES, SC_TILE), F32),
        ],
        compiler_params=_params("arbitrary", "arbitrary"),
        name="in_proj",
    )(x, n1w, w, wba, cwq, cws, alog, dtb, ltri)


GDN_CHUNKS_PER_TRIP = 4


def _sdot(a, b):
    return _dot(a.astype(BF16), b.astype(BF16))


def _gdn_kernel(q_ref, k_ref, v_ref, z_ref, bg_ref, gt_ref, nw_ref, out_ref, state_sc, *, tc):
    @pl.when(pl.program_id(1) == 0)
    def _():
        state_sc[...] = jnp.zeros_like(state_sc)

    row = lax.broadcasted_iota(I32, (CHUNK, CHUNK), 0)
    col = lax.broadcasted_iota(I32, (CHUNK, CHUNK), 1)
    causal = row >= col
    strict = row > col
    eye = (row == col).astype(F32)
    same16 = (row // 16) == (col // 16)
    same32 = (row // 32) == (col // 32)
    m16 = same16.astype(F32)
    m32 = (same32 & jnp.logical_not(same16)).astype(F32)
    m64 = jnp.logical_not(same32).astype(F32)
    nw = nw_ref[...]
    tt = (((0,), (0,)), ((), ()))
    nt = (((1,), (1,)), ((), ()))
    heads = range(GDN_HEADS)

    def hcols(h):
        return slice(h * HEAD_DIM, (h + 1) * HEAD_DIM)

    def trip_body(t, carry):
        units = []
        for cc in range(GDN_CHUNKS_PER_TRIP):
            c = t * GDN_CHUNKS_PER_TRIP + cc
            rows = pl.ds(pl.multiple_of(c * CHUNK, CHUNK), CHUNK)
            bgc = bg_ref[rows, :]
            gt = gt_ref[c]
            for h in heads:
                gc = bgc[:, GDN_HEADS + h:GDN_HEADS + h + 1]
                units.append(dict(
                    rows=rows, h=h, gc=gc,
                    beta=bgc[:, h:h + 1],
                    gcr=gt[GDN_HEADS + h:GDN_HEADS + h + 1, :],
                    glast=gc[CHUNK - 1:CHUNK, :],
                    q=q_ref[rows, hcols(h)], k=k_ref[rows, hcols(h)]))

        for u in units:
            u["kq"] = lax.dot_general(jnp.concatenate([u["k"], u["q"]], axis=0), u["k"], nt,
                                      preferred_element_type=F32)
        for u in units:
            decay = jnp.where(causal, jnp.exp(u["gc"] - u["gcr"]), 0.0)
            u["a"] = jnp.where(strict, u["kq"][:CHUNK] * u["beta"] * decay, 0.0)
            u["attn"] = (u["kq"][CHUNK:] * decay).astype(BF16)

        for u in units:
            u["n"] = -(u["a"] * m16)
            u["p"] = eye + u["n"]
        for _ in range(3):
            for u in units:
                u["n"] = _sdot(u["n"], u["n"])
            for u in units:
                u["p"] = u["p"] + _sdot(u["p"], u["n"])
        for m in (m32, m64):
            for u in units:
                u["e"] = _sdot(u["a"] * m, u["p"])
            for u in units:
                u["p"] = u["p"] - _sdot(u["p"], u["e"])

        for u in units:
            u["eg"] = jnp.exp(u["gc"])
            u["kf"] = u["k"].astype(F32)
            v = v_ref[u["rows"], hcols(u["h"])].astype(F32)
            bw = u["beta"] * u["eg"]
            rhs = jnp.concatenate([v * u["beta"], u["kf"] * bw], axis=1)
            u["uw"] = _sdot(u["p"], rhs)

        state = [state_sc[h] for h in heads]
        for cc in range(GDN_CHUNKS_PER_TRIP):
            us = units[cc * GDN_HEADS:(cc + 1) * GDN_HEADS]
            for u in us:
                wq = jnp.concatenate([u["uw"][:, HEAD_DIM:], u["q"].astype(F32) * u["eg"]], axis=0)
                u["rs"] = _dot(wq.astype(BF16), state[u["h"]].astype(BF16))
            for u in us:
                u["vb"] = (u["uw"][:, :HEAD_DIM] - u["rs"][:CHUNK]).astype(BF16)
                u["o"] = u["rs"][CHUNK:] + _dot(u["attn"], u["vb"])
            for u in us:
                k_dec = (u["kf"] * jnp.exp(u["glast"] - u["gc"])).astype(BF16)
                state[u["h"]] = state[u["h"]] * jnp.exp(u["glast"]) + lax.dot_general(
                    k_dec, u["vb"], tt, preferred_element_type=F32)
            for u in us:
                z = z_ref[u["rows"], hcols(u["h"])].astype(F32)
                out_ref[u["rows"], hcols(u["h"])] = (
                    _rms_scale(u["o"]) * nw * _silu(z)).astype(BF16)
        for h in heads:
            state_sc[h] = state[h]
        return carry

    lax.fori_loop(0, tc // (CHUNK * GDN_CHUNKS_PER_TRIP), trip_body, 0)


def _gdn_call(qkvz, bg, gt, nw, tc):
    bsz, seq, _ = qkvz.shape

    def col(j):
        return pl.BlockSpec((None, tc, GDN_WIDTH), lambda b, i: (b, i, j))

    return pl.pallas_call(
        functools.partial(_gdn_kernel, tc=tc),
        grid=(bsz, seq // tc),
        in_specs=[
            col(0), col(1), col(2), col(3),
            pl.BlockSpec((None, tc, LANES), lambda b, i: (b, i, 0)),
            pl.BlockSpec((None, tc // CHUNK, 2 * GDN_HEADS, CHUNK), lambda b, i: (b, i, 0, 0)),
            _resident((1, HEAD_DIM)),
        ],
        out_specs=pl.BlockSpec((None, tc, GDN_WIDTH), lambda b, i: (b, i, 0)),
        out_shape=jax.ShapeDtypeStruct((bsz, seq, GDN_WIDTH), BF16),
        scratch_shapes=[pltpu.VMEM((GDN_HEADS, HEAD_DIM, HEAD_DIM), F32)],
        compiler_params=_params("arbitrary", "arbitrary"),
        name="gated_delta_rule",
    )(qkvz, qkvz, qkvz, qkvz, bg, gt, nw)


def _pack_bf16_pair(hi, lo):
    hb = lax.bitcast_convert_type(hi.astype(BF16).astype(F32), U32)
    lb = lax.bitcast_convert_type(lo.astype(BF16).astype(F32), U32)
    return (hb & jnp.uint32(0xFFFF0000)) | lax.shift_right_logical(lb, jnp.uint32(16))


def _unpack_bf16_pair(p):
    hi = lax.bitcast_convert_type(p & jnp.uint32(0xFFFF0000), F32).astype(BF16)
    lo = lax.bitcast_convert_type(lax.shift_left(p, jnp.uint32(16)), F32).astype(BF16)
    return hi, lo


def _outproj_kernel(yg_ref, ys_ref, x_ref, wo_ref, n2w_ref, wr_ref, rb_ref,
                    h1_ref, hp_ref, lt_ref):
    h1 = (x_ref[...] + _dot(yg_ref[...], wo_ref[0:GDN_WIDTH, :])
          + _dot(ys_ref[...], wo_ref[GDN_WIDTH:GDN_WIDTH + SC_WIDTH, :]))
    h1_ref[...] = h1
    hn = _rms_scale(h1) * n2w_ref[...]
    hn_hi = hn.astype(BF16)
    hn_lo = (hn - hn_hi.astype(F32)).astype(BF16)
    lg = _dot(hn_hi, wr_ref[...])
    lg = lg[:, :LANES] + lg[:, LANES:] + _dot(hn_lo, wr_ref[:, :LANES])
    lt_ref[...] = lg.T + rb_ref[...]
    half = D_MODEL // 2
    hp_ref[...] = _pack_bf16_pair(hn[:, :half], hn[:, half:])


def _outproj_call(yg, ys, x2d, wo, n2w, wr_t, rb, tm):
    n_tok, d = x2d.shape
    return pl.pallas_call(
        _outproj_kernel,
        grid=(n_tok // tm,),
        in_specs=[
            pl.BlockSpec((tm, GDN_WIDTH), lambda i: (i, 0)),
            pl.BlockSpec((tm, SC_WIDTH), lambda i: (i, 0)),
            pl.BlockSpec((tm, d), lambda i: (i, 0)),
            _resident((GDN_WIDTH + SC_WIDTH, d)),
            _resident((1, d)),
            _resident((d, 2 * LANES)),
            _resident((LANES, 1)),
        ],
        out_specs=[
            pl.BlockSpec((tm, d), lambda i: (i, 0)),
            pl.BlockSpec((tm, d // 2), lambda i: (i, 0)),
            pl.BlockSpec((LANES, tm), lambda i: (0, i)),
        ],
        out_shape=[
            jax.ShapeDtypeStruct((n_tok, d), F32),
            jax.ShapeDtypeStruct((n_tok, d // 2), U32),
            jax.ShapeDtypeStruct((LANES, n_tok), F32),
        ],
        compiler_params=_params("arbitrary"),
        name="out_proj_router",
    )(yg, ys, x2d, wo, n2w, wr_t, rb)


def _first_argmax(vals, iota, size):
    m = jnp.max(vals, axis=0, keepdims=True)
    idx = jnp.min(jnp.where(vals == m, iota, size), axis=0, keepdims=True)
    return m, idx


def _route_kernel(lt_ref, idx_ref, gate_ref, cnt_ref, carry_sc, *, tr):
    @pl.when(pl.program_id(0) == 0)
    def _():
        carry_sc[...] = jnp.zeros_like(carry_sc)

    iota8 = lax.broadcasted_iota(I32, (EXPERTS_PER_GROUP, tr), 0)
    gl = lt_ref[0:N_GROUPS, :]
    gmax, gidx = _first_argmax(gl, iota8, N_GROUPS)
    g_w = 1.0 / jnp.sum(jnp.exp(gl - gmax), axis=0, keepdims=True)

    ig = jnp.zeros((EXPERTS_PER_GROUP, tr), F32)
    for g in range(N_GROUPS):
        lo = N_GROUPS + g * EXPERTS_PER_GROUP
        ig = jnp.where(gidx == g, lt_ref[lo:lo + EXPERTS_PER_GROUP, :], ig)
    pe = jnp.exp(ig - jnp.max(ig, axis=0, keepdims=True))
    p = pe / jnp.sum(pe, axis=0, keepdims=True)
    p1, i1 = _first_argmax(p, iota8, EXPERTS_PER_GROUP)
    p2, i2 = _first_argmax(jnp.where(iota8 == i1, -1.0, p), iota8, EXPERTS_PER_GROUP)
    denom = p1 + p2
    e1 = gidx * EXPERTS_PER_GROUP + i1
    e2 = gidx * EXPERTS_PER_GROUP + i2

    iota_e = lax.broadcasted_iota(I32, (N_EXPERTS, tr), 0)
    oh1 = (iota_e == e1).astype(F32)
    oh2 = (iota_e == e2).astype(F32)
    ohs = oh1 + oh2
    r = lax.broadcasted_iota(I32, (tr, tr), 0)
    c = lax.broadcasted_iota(I32, (tr, tr), 1)
    before = jnp.where(r < c, 1.0, 0.0).astype(BF16)
    base = carry_sc[:, 0:1] + _dot(ohs.astype(BF16), before)
    rank1 = jnp.sum(oh1 * base, axis=0, keepdims=True)
    rank2 = jnp.sum(oh2 * base, axis=0, keepdims=True)
    carry_sc[...] = carry_sc[...] + jnp.sum(ohs, axis=1, keepdims=True)
    cnt_ref[...] = carry_sc[...]

    idx_ref[...] = jnp.zeros_like(idx_ref)
    idx_ref[0:1, :] = e1
    idx_ref[1:2, :] = e2
    idx_ref[2:3, :] = rank1.astype(I32)
    idx_ref[3:4, :] = rank2.astype(I32)
    sub = lax.broadcasted_iota(I32, (LANES, tr), 0)
    gmat = jnp.where(sub == 0, g_w * p1 / denom, jnp.where(sub == 1, g_w * p2 / denom, 0.0))
    gate_ref[...] = gmat.T


def _route_call(lt, tr):
    n_tok = lt.shape[1]
    return pl.pallas_call(
        functools.partial(_route_kernel, tr=tr),
        grid=(n_tok // tr,),
        in_specs=[pl.BlockSpec((LANES, tr), lambda i: (0, i))],
        out_specs=[
            pl.BlockSpec((SUBLANES, tr), lambda i: (0, i)),
            pl.BlockSpec((tr, LANES), lambda i: (i, 0)),
            pl.BlockSpec((N_EXPERTS, LANES), lambda i: (0, 0)),
        ],
        out_shape=[
            jax.ShapeDtypeStruct((SUBLANES, n_tok), I32),
            jax.ShapeDtypeStruct((n_tok, LANES), F32),
            jax.ShapeDtypeStruct((N_EXPERTS, LANES), F32),
        ],
        scratch_shapes=[pltpu.VMEM((N_EXPERTS, LANES), F32)],
        compiler_params=_params("arbitrary"),
        name="moe_route",
    )(lt)


DISPATCH_SLOTS = 3


def _dispatch_kernel(zf_ref, d1_ref, d2_ref, d1p_ref, d2p_ref, hp_ref, xs_ref,
                     stage, zbuf, sem_in, sem_out, sem_z, *, tb, bm):
    i = pl.program_id(0)
    last = pl.num_programs(0) - 1

    def block_load(step):
        s = step % DISPATCH_SLOTS
        return pltpu.make_async_copy(hp_ref.at[pl.ds(step * tb, tb)], stage.at[s], sem_in.at[s])

    def row_copies(step, da_ref, db_ref):
        s = step % DISPATCH_SLOTS
        return [pltpu.make_async_copy(stage.at[s, pl.ds(u, 1)],
                                      xs_ref.at[pl.ds(d_ref[0, 0, u], 1)], sem_out.at[s])
                for u in range(tb) for d_ref in (da_ref, db_ref)]

    def zero_fill(b):
        return pltpu.make_async_copy(zbuf, xs_ref.at[pl.ds(pl.multiple_of(b * bm, bm), bm)], sem_z)

    @pl.when(i == 0)
    def _():
        block_load(i).start()
        zbuf[...] = jnp.zeros_like(zbuf)

        def start(b, carry):
            @pl.when(zf_ref[b] != 0)
            def _():
                zero_fill(b).start()
            return carry

        def wait(b, carry):
            @pl.when(zf_ref[b] != 0)
            def _():
                zero_fill(b).wait()
            return carry

        lax.fori_loop(0, zf_ref.shape[0], start, 0)
        lax.fori_loop(0, zf_ref.shape[0], wait, 0)

    @pl.when(i < last)
    def _():
        block_load(i + 1).start()

    block_load(i).wait()
    for cp in row_copies(i, d1_ref, d2_ref):
        cp.start()

    @pl.when(i > 0)
    def _():
        for cp in row_copies(i - 1, d1p_ref, d2p_ref):
            cp.wait()

    @pl.when(i == last)
    def _():
        for cp in row_copies(i, d1_ref, d2_ref):
            cp.wait()


def _dispatch_call(zero_flag, d1, d2, hp, n_rows, tb, bm):
    n_tok, width = hp.shape
    cur = pl.BlockSpec((1, 1, tb), lambda i, zf: (i, 0, 0), memory_space=pltpu.SMEM)
    prev = pl.BlockSpec((1, 1, tb), lambda i, zf: (jnp.maximum(i - 1, 0), 0, 0),
                        memory_space=pltpu.SMEM)
    d1 = d1.reshape(n_tok // tb, 1, tb)
    d2 = d2.reshape(n_tok // tb, 1, tb)
    grid_spec = pltpu.PrefetchScalarGridSpec(
        num_scalar_prefetch=1,
        grid=(n_tok // tb,),
        in_specs=[cur, cur, prev, prev, pl.BlockSpec(memory_space=pl.ANY)],
        out_specs=pl.BlockSpec(memory_space=pl.ANY),
        scratch_shapes=[
            pltpu.VMEM((DISPATCH_SLOTS, tb, width), U32),
            pltpu.VMEM((bm, width), U32),
            pltpu.SemaphoreType.DMA((DISPATCH_SLOTS,)),
            pltpu.SemaphoreType.DMA((DISPATCH_SLOTS,)),
            pltpu.SemaphoreType.DMA(()),
        ],
    )
    return pl.pallas_call(
        functools.partial(_dispatch_kernel, tb=tb, bm=bm),
        grid_spec=grid_spec,
        out_shape=jax.ShapeDtypeStruct((n_rows, width), U32),
        compiler_params=_params("arbitrary"),
        name="moe_dispatch",
    )(zero_flag, d1, d2, d1, d2, hp)


def _expert_kernel(be_ref, nu_ref, first_ref, nxt_ref, slot_ref, xs_ref, wg_hbm, wu_hbm, wd_hbm,
                   ys_ref, wg32, wu32, wd32, wgb, wub, wdb, sem):
    i = pl.program_id(0)
    used = i < nu_ref[0]

    def weight_copies(e, s):
        return [pltpu.make_async_copy(src.at[e], dst.at[s], sem.at[s])
                for src, dst in ((wg_hbm, wg32), (wu_hbm, wu32), (wd_hbm, wd32))]

    @pl.when(used & (first_ref[i] != 0))
    def _():
        e = be_ref[i]
        s = slot_ref[i]

        @pl.when(i == 0)
        def _():
            for cp in weight_copies(e, s):
                cp.start()

        for cp in weight_copies(e, s):
            cp.wait()

        @pl.when(nxt_ref[i] >= 0)
        def _():
            for cp in weight_copies(nxt_ref[i], 1 - s):
                cp.start()

        wgb[...] = wg32[s].astype(BF16)
        wub[...] = wu32[s].astype(BF16)
        wdb[...] = wd32[s].astype(BF16)

    @pl.when(used)
    def _():
        hi, lo = _unpack_bf16_pair(xs_ref[...])
        xb = jnp.concatenate([hi, lo], axis=1)
        hid = _silu(_dot(xb, wgb[...])) * _dot(xb, wub[...])
        y = _dot(hid.astype(BF16), wdb[...])
        ys_ref[...] = _pack_bf16_pair(y[:, :D_MODEL // 2], y[:, D_MODEL // 2:])

    @pl.when(jnp.logical_not(used))
    def _():
        ys_ref[...] = jnp.zeros_like(ys_ref)


def _expert_call(block_expert, n_used, first, nxt, slot, xs, wg, wu, wd, bm):
    n_rows, width = xs.shape
    d = 2 * width
    nb = n_rows // bm
    grid_spec = pltpu.PrefetchScalarGridSpec(
        num_scalar_prefetch=5,
        grid=(nb,),
        in_specs=[
            pl.BlockSpec((bm, width), lambda i, be, nu, *_: (jnp.minimum(i, nu[0] - 1), 0)),
            pl.BlockSpec(memory_space=pl.ANY),
            pl.BlockSpec(memory_space=pl.ANY),
            pl.BlockSpec(memory_space=pl.ANY),
        ],
        out_specs=pl.BlockSpec((bm, width), lambda i, *_: (i, 0)),
        scratch_shapes=[
            pltpu.VMEM((2, d, D_EXPERT), F32),
            pltpu.VMEM((2, d, D_EXPERT), F32),
            pltpu.VMEM((2, D_EXPERT, d), F32),
            pltpu.VMEM((d, D_EXPERT), BF16),
            pltpu.VMEM((d, D_EXPERT), BF16),
            pltpu.VMEM((D_EXPERT, d), BF16),
            pltpu.SemaphoreType.DMA((2,)),
        ],
    )
    return pl.pallas_call(
        _expert_kernel,
        grid_spec=grid_spec,
        out_shape=jax.ShapeDtypeStruct((n_rows, width), U32),
        compiler_params=_params("arbitrary"),
        name="moe_experts",
    )(block_expert, n_used, first, nxt, slot, xs, wg, wu, wd)


def _combine_kernel(d1_ref, d2_ref, d1n_ref, d2n_ref, gate_ref, h1_ref, wf_ref, ys_ref, out_ref,
                    buf, sem, *, tc):
    i = pl.program_id(0)
    slot = i % 2

    def row_copies(da_ref, db_ref, s):
        return [pltpu.make_async_copy(ys_ref.at[pl.ds(d_ref[0, 0, u], 1)],
                                      buf.at[s, k, pl.ds(u, 1)], sem.at[s])
                for u in range(tc) for k, d_ref in enumerate((da_ref, db_ref))]

    @pl.when(i == 0)
    def _():
        for cp in row_copies(d1_ref, d2_ref, 0):
            cp.start()

    @pl.when(i + 1 < pl.num_programs(0))
    def _():
        for cp in row_copies(d1n_ref, d2n_ref, 1 - slot):
            cp.start()

    for cp in row_copies(d1_ref, d2_ref, slot):
        cp.wait()
    gates = gate_ref[...]
    h = h1_ref[...]
    for k in range(TOP_K):
        hi, lo = _unpack_bf16_pair(buf[slot, k])
        y = jnp.concatenate([hi, lo], axis=1).astype(F32)
        h = h + y * gates[:, k:k + 1]
    out_ref[...] = _rms_scale(h) * wf_ref[...]


def _combine_call(d1, d2, gates, h1, wf, ys, tc):
    n_tok, d = h1.shape
    steps = n_tok // tc
    cur = pl.BlockSpec((1, 1, tc), lambda i: (i, 0, 0), memory_space=pltpu.SMEM)
    nxt = pl.BlockSpec((1, 1, tc), lambda i: (jnp.minimum(i + 1, steps - 1), 0, 0),
                       memory_space=pltpu.SMEM)
    d1 = d1.reshape(steps, 1, tc)
    d2 = d2.reshape(steps, 1, tc)
    return pl.pallas_call(
        functools.partial(_combine_kernel, tc=tc),
        grid=(steps,),
        in_specs=[
            cur, cur, nxt, nxt,
            pl.BlockSpec((tc, LANES), lambda i: (i, 0)),
            pl.BlockSpec((tc, d), lambda i: (i, 0)),
            _resident((1, d)),
            pl.BlockSpec(memory_space=pl.ANY),
        ],
        out_specs=pl.BlockSpec((tc, d), lambda i: (i, 0)),
        out_shape=jax.ShapeDtypeStruct((n_tok, d), F32),
        scratch_shapes=[
            pltpu.VMEM((2, TOP_K, tc, d // 2), U32),
            pltpu.SemaphoreType.DMA((2,)),
        ],
        compiler_params=_params("arbitrary"),
        name="moe_combine",
    )(d1, d2, d1, d2, gates, h1, wf, ys)


def _tiles(seq, n_tok):
    return dict(
        proj=min(256, seq),
        gdn=min(512, seq),
        outp=min(512, n_tok),
        route=min(512, n_tok),
        disp=min(512, n_tok),
        bm=256,
        comb=min(512, n_tok),
    )


def _table_lookup(table, keys):
    e_ids = jnp.arange(table.shape[0], dtype=I32)
    return jnp.sum(jnp.where(e_ids[:, None] == keys[None, :], table[:, None], 0), axis=0)


def _layer(h, n1w, w_in, conv_qkv_w, a_log, dt_bias, gdn_norm_w, conv_sc_w, w_out,
           n2w, wg_r, bg_r, we_r, be_r, w_gate, w_up, w_down, final_w, last):
    bsz, seq, d = h.shape
    n_tok = bsz * seq
    ts = _tiles(seq, n_tok)

    qkvz_cols = 4 * GDN_WIDTH
    sc0 = qkvz_cols + 2 * GDN_HEADS
    w_main = jnp.concatenate([w_in[:, :qkvz_cols], w_in[:, sc0:]], axis=1).astype(BF16)
    w_ba = jnp.pad(w_in[:, qkvz_cols:sc0], ((0, 0), (0, LANES - 2 * GDN_HEADS))).astype(BF16)
    pad8 = (GDN_HEADS, LANES - 2 * GDN_HEADS)
    alog_v = jnp.pad(a_log, pad8).reshape(1, LANES)
    dtb_v = jnp.pad(dt_bias, pad8).reshape(1, LANES)

    qkvz, y_sc, bg = _inproj_call(h, n1w.reshape(1, d), w_main, w_ba, conv_qkv_w, conv_sc_w,
                                  alog_v, dtb_v, ts["proj"])

    gt = jnp.swapaxes(bg[:, :, :2 * GDN_HEADS].reshape(bsz, seq // CHUNK, CHUNK, 2 * GDN_HEADS),
                      2, 3)
    y_gdn = _gdn_call(qkvz, bg, gt, gdn_norm_w.reshape(1, HEAD_DIM), ts["gdn"])

    n_router = N_GROUPS + N_EXPERTS
    wr = jnp.pad(jnp.concatenate([wg_r, we_r], axis=1), ((0, 0), (0, LANES - n_router)))
    wr_hi = wr.astype(BF16)
    wr_t = jnp.concatenate([wr_hi, (wr - wr_hi.astype(F32)).astype(BF16)], axis=1)
    rb = jnp.pad(jnp.concatenate([bg_r, be_r]), (0, LANES - n_router)).reshape(LANES, 1)
    x2d = h.reshape(n_tok, d)
    h1, hp, lt = _outproj_call(y_gdn.reshape(n_tok, GDN_WIDTH), y_sc.reshape(n_tok, SC_WIDTH),
                               x2d, w_out.astype(BF16), n2w.reshape(1, d), wr_t, rb, ts["outp"])

    idx, gates, cnt = _route_call(lt, ts["route"])
    bm = ts["bm"]
    counts = cnt[:, 0].astype(I32)
    padded = (counts + bm - 1) // bm * bm
    pad_end = jnp.cumsum(padded)
    pad_start = pad_end - padded
    d1 = _table_lookup(pad_start, idx[0]) + idx[2]
    d2 = _table_lookup(pad_start, idx[1]) + idx[3]
    nb = n_tok * TOP_K // bm + N_EXPERTS
    block_row = jnp.arange(nb, dtype=I32) * bm
    block_expert = jnp.minimum(
        jnp.sum((pad_end[None, :] <= block_row[:, None]).astype(I32), axis=1), N_EXPERTS - 1)
    n_used = (pad_end[-1:] // bm).astype(I32)

    has_rows = padded > 0
    e_ids = jnp.arange(N_EXPERTS, dtype=I32)
    later = jnp.where(has_rows, e_ids, N_EXPERTS)
    next_used = jnp.concatenate([lax.cummin(later, reverse=True)[1:],
                                 jnp.full((1,), N_EXPERTS, I32)])
    next_used = jnp.where(next_used < N_EXPERTS, next_used, -1)
    rank_used = jnp.cumsum(has_rows.astype(I32)) - 1
    first = (block_row == _table_lookup(pad_start, block_expert)).astype(I32)
    nxt = _table_lookup(next_used, block_expert)
    slot = _table_lookup(rank_used, block_expert) % 2
    is_last = jnp.any((pad_end[None, :] == (block_row + bm)[:, None]) & has_rows[None, :], axis=1)
    zero_flag = (is_last | (block_row >= pad_end[-1])).astype(I32)

    xs = _dispatch_call(zero_flag, d1, d2, hp, nb * bm, ts["disp"], bm)
    ys = _expert_call(block_expert, n_used, first, nxt, slot, xs, w_gate, w_up, w_down, bm)
    out = _combine_call(d1, d2, gates, h1, final_w.reshape(1, d), ys, ts["comb"])
    del last
    return out.reshape(bsz, seq, d)


def kernel(x, norm1_w, w_in, conv_qkv_w, a_log, dt_bias, gdn_norm_w, conv_sc_w, w_out, norm2_w,
           router_group_w, router_group_b, router_expert_w, router_expert_b, w_gate, w_up,
           w_down, final_norm_w):
    depth = norm1_w.shape[0]
    assert depth == 1, "the combine call applies the final RMSNorm, so a single layer is assumed"
    return _layer(x, norm1_w[0], w_in[0], conv_qkv_w[0], a_log[0], dt_bias[0], gdn_norm_w[0],
                  conv_sc_w[0], w_out[0], norm2_w[0], router_group_w[0], router_group_b[0],
                  router_expert_w[0], router_expert_b[0], w_gate[0], w_up[0], w_down[0],
                  final_norm_w, True)
```

```python
import functools

import jax
import jax.numpy as jnp
from jax import lax
from jax.experimental import pallas as pl
from jax.experimental.pallas import tpu as pltpu

F32 = jnp.float32
BF16 = jnp.bfloat16
I32 = jnp.int32
U32 = jnp.uint32
HIGHEST = lax.Precision.HIGHEST

EPS = 1e-6
D_MODEL = 2048
HEAD_DIM = 128
GDN_HEADS = 8
GDN_WIDTH = GDN_HEADS * HEAD_DIM
GDN_CONV = 4
CHUNK = 64
SC_WIDTH = 1024
SC_CONV = 3
N_GROUPS = 8
EXPERTS_PER_GROUP = 8
N_EXPERTS = N_GROUPS * EXPERTS_PER_GROUP
TOP_K = 2
D_EXPERT = 512
LANES = 128
SUBLANES = 8
SC_TILE = 256
VMEM_LIMIT = 56 * 1024 * 1024


def _sigmoid(x):
    return 1.0 / (1.0 + jnp.exp(-x))


def _silu(x):
    return x * _sigmoid(x)


def _softplus(x):
    return jnp.maximum(x, 0.0) + jnp.log1p(jnp.exp(-jnp.abs(x)))


def _rms_scale(x):
    return x * lax.rsqrt(jnp.mean(x * x, axis=-1, keepdims=True) + EPS)


def _dot(a, b):
    return jnp.dot(a, b, preferred_element_type=F32)


def _hdot(a, b):
    return jnp.dot(a, b, preferred_element_type=F32, precision=HIGHEST)


def _params(*sem):
    return pltpu.CompilerParams(dimension_semantics=sem, vmem_limit_bytes=VMEM_LIMIT)


ROW_PIECE = 64


def _conv_piece(x, prev, cw):
    taps = cw.shape[0]
    rows = x.shape[0]
    ext = jnp.concatenate([prev, x], axis=0)
    y = cw[taps - 1:taps, :] * x
    for s in range(1, taps):
        shifted = pltpu.roll(ext, s, axis=0)[SUBLANES:SUBLANES + rows, :]
        y = y + cw[taps - 1 - s:taps - s, :] * shifted
    return y


def _resident(shape):
    return pl.BlockSpec(shape, lambda *_: (0,) * len(shape), pipeline_mode=pl.Buffered(1))


SC_COL0 = 4 * GDN_WIDTH
PROJ_ORDER = (("qkvz", 0), ("sc", 0), ("qkvz", 1), ("sc", 1), ("qkvz", 2), ("sc", 2),
              ("qkvz", 3), ("sc", 3))


def _inproj_kernel(x_ref, n1w_ref, w_ref, wba_ref, cwq_ref, cws_ref, alog_ref, dtb_ref, ltri_ref,
                   qkvz_ref, ysc_ref, bg_ref, carry_q, carry_s, *, tm):
    @pl.when(pl.program_id(1) == 0)
    def _():
        carry_q[...] = jnp.zeros_like(carry_q)
        carry_s[...] = jnp.zeros_like(carry_s)

    hn = (_rms_scale(x_ref[...]) * n1w_ref[...]).astype(BF16)

    ba = _dot(hn, wba_ref[...])
    lane = lax.broadcasted_iota(I32, ba.shape, 1)
    g = -jnp.exp(alog_ref[...]) * _softplus(ba + dtb_ref[...])
    g = jnp.where((lane >= GDN_HEADS) & (lane < 2 * GDN_HEADS), g, 0.0)
    ltri = ltri_ref[...]
    gc = jnp.concatenate([_hdot(ltri, g[c * CHUNK:(c + 1) * CHUNK, :])
                          for c in range(tm // CHUNK)], axis=0)
    bg_ref[...] = jnp.where(lane < GDN_HEADS, _sigmoid(ba), gc)

    def epilogue(kind, j, p):
        pieces = [(r * ROW_PIECE, c * LANES) for r in range(tm // ROW_PIECE)
                  for c in range((SC_TILE if kind == "sc" else GDN_WIDTH) // LANES)]
        tails = {}
        for r0, c0 in pieces:
            rows = slice(r0, r0 + ROW_PIECE)
            lanes = slice(c0, c0 + LANES)
            if kind == "sc":
                out_cols = slice(SC_TILE * j + c0, SC_TILE * j + c0 + LANES)
                x = p[rows, SC_TILE + c0:SC_TILE + c0 + LANES] * p[
                    rows, 2 * SC_TILE + c0:2 * SC_TILE + c0 + LANES]
                prev = tails.get(c0, carry_s[j, :, lanes]) if r0 else carry_s[j, :, lanes]
                tails[c0] = x[ROW_PIECE - SUBLANES:, :]
                y = _conv_piece(x, prev, cws_ref[:, out_cols])
                ysc_ref[rows, out_cols] = (p[rows, lanes] * y).astype(BF16)
                continue
            out_cols = slice(j * GDN_WIDTH + c0, j * GDN_WIDTH + c0 + LANES)
            x = p[rows, lanes]
            if j == 3:
                qkvz_ref[rows, out_cols] = x.astype(BF16)
                continue
            prev = tails.get(c0, carry_q[j, :, lanes]) if r0 else carry_q[j, :, lanes]
            tails[c0] = x[ROW_PIECE - SUBLANES:, :]
            y = _silu(_conv_piece(x, prev, cwq_ref[:, out_cols]))
            if j < 2:
                scale = HEAD_DIM ** -0.5 if j == 0 else 1.0
                y = y * (lax.rsqrt(jnp.sum(y * y, axis=-1, keepdims=True) + EPS) * scale)
            qkvz_ref[rows, out_cols] = y.astype(BF16)
        carry = carry_s if kind == "sc" else carry_q
        for c0, tail in tails.items():
            carry[j, :, c0:c0 + LANES] = tail

    for kind, j in PROJ_ORDER:
        if kind == "qkvz":
            p = _dot(hn, w_ref[:, j * GDN_WIDTH:(j + 1) * GDN_WIDTH])
        else:
            starts = [SC_COL0 + part * SC_WIDTH + SC_TILE * j for part in range(3)]
            p = jnp.concatenate([_dot(hn, w_ref[:, lo:lo + SC_TILE]) for lo in starts],
                                axis=1)
        epilogue(kind, j, p)


def _inproj_call(x, n1w, w, wba, cwq, cws, alog, dtb, tm):
    bsz, seq, d = x.shape
    idx = jnp.arange(CHUNK)
    ltri = (idx[:, None] >= idx[None, :]).astype(F32)

    def rows(width):
        return pl.BlockSpec((None, tm, width), lambda b, i: (b, i, 0))

    return pl.pallas_call(
        functools.partial(_inproj_kernel, tm=tm),
        grid=(bsz, seq // tm),
        in_specs=[
            rows(d),
            _resident((1, d)),
            _resident((d, SC_COL0 + 3 * SC_WIDTH)),
            _resident((d, LANES)),
            _resident((GDN_CONV, 3 * GDN_WIDTH)),
            _resident((SC_CONV, SC_WIDTH)),
            _resident((1, LANES)),
            _resident((1, LANES)),
            _resident((CHUNK, CHUNK)),
        ],
        out_specs=[rows(4 * GDN_WIDTH), rows(SC_WIDTH), rows(LANES)],
        out_shape=[
            jax.ShapeDtypeStruct((bsz, seq, 4 * GDN_WIDTH), BF16),
            jax.ShapeDtypeStruct((bsz, seq, SC_WIDTH), BF16),
            jax.ShapeDtypeStruct((bsz, seq, LANES), F32),
        ],
        scratch_shapes=[
            pltpu.VMEM((3, SUBLANES, GDN_WIDTH), F32),
            pltpu.VMEM((SC_WIDTH // SC_TILE, SUBLANES, SC_TILE), F32),
        ],
        compiler_params=_params("arbitrary", "arbitrary"),
        name="in_proj",
    )(x, n1w, w, wba, cwq, cws, alog, dtb, ltri)


GDN_CHUNKS_PER_TRIP = 4


def _sdot(a, b):
    return _dot(a.astype(BF16), b.astype(BF16))


def _gdn_kernel(q_ref, k_ref, v_ref, z_ref, bg_ref, gt_ref, nw_ref, out_ref, state_sc, *, tc):
    @pl.when(pl.program_id(1) == 0)
    def _():
        state_sc[...] = jnp.zeros_like(state_sc)

    row = lax.broadcasted_iota(I32, (CHUNK, CHUNK), 0)
    col = lax.broadcasted_iota(I32, (CHUNK, CHUNK), 1)
    causal = row >= col
    strict = row > col
    eye = (row == col).astype(F32)
    same16 = (row // 16) == (col // 16)
    same32 = (row // 32) == (col // 32)
    m16 = same16.astype(F32)
    m32 = (same32 & jnp.logical_not(same16)).astype(F32)
    m64 = jnp.logical_not(same32).astype(F32)
    nw = nw_ref[...]
    tt = (((0,), (0,)), ((), ()))
    nt = (((1,), (1,)), ((), ()))
    heads = range(GDN_HEADS)

    def hcols(h):
        return slice(h * HEAD_DIM, (h + 1) * HEAD_DIM)

    def trip_body(t, carry):
        units = []
        for cc in range(GDN_CHUNKS_PER_TRIP):
            c = t * GDN_CHUNKS_PER_TRIP + cc
            rows = pl.ds(pl.multiple_of(c * CHUNK, CHUNK), CHUNK)
            bgc = bg_ref[rows, :]
            gt = gt_ref[c]
            for h in heads:
                gc = bgc[:, GDN_HEADS + h:GDN_HEADS + h + 1]
                units.append(dict(
                    rows=rows, h=h, gc=gc,
                    beta=bgc[:, h:h + 1],
                    gcr=gt[GDN_HEADS + h:GDN_HEADS + h + 1, :],
                    glast=gc[CHUNK - 1:CHUNK, :],
                    q=q_ref[rows, hcols(h)], k=k_ref[rows, hcols(h)]))

        for u in units:
            u["kq"] = lax.dot_general(jnp.concatenate([u["k"], u["q"]], axis=0), u["k"], nt,
                                      preferred_element_type=F32)
        for u in units:
            decay = jnp.where(causal, jnp.exp(u["gc"] - u["gcr"]), 0.0)
            u["a"] = jnp.where(strict, u["kq"][:CHUNK] * u["beta"] * decay, 0.0)
            u["attn"] = (u["kq"][CHUNK:] * decay).astype(BF16)

        for u in units:
            u["n"] = -(u["a"] * m16)
            u["p"] = eye + u["n"]
        for _ in range(3):
            for u in units:
                u["n"] = _sdot(u["n"], u["n"])
            for u in units:
                u["p"] = u["p"] + _sdot(u["p"], u["n"])
        for m in (m32, m64):
            for u in units:
                u["e"] = _sdot(u["a"] * m, u["p"])
            for u in units:
                u["p"] = u["p"] - _sdot(u["p"], u["e"])

        for u in units:
            u["eg"] = jnp.exp(u["gc"])
            u["kf"] = u["k"].astype(F32)
            v = v_ref[u["rows"], hcols(u["h"])].astype(F32)
            bw = u["beta"] * u["eg"]
            rhs = jnp.concatenate([v * u["beta"], u["kf"] * bw], axis=1)
            u["uw"] = _sdot(u["p"], rhs)

        state = [state_sc[h] for h in heads]
        for cc in range(GDN_CHUNKS_PER_TRIP):
            us = units[cc * GDN_HEADS:(cc + 1) * GDN_HEADS]
            for u in us:
                wq = jnp.concatenate([u["uw"][:, HEAD_DIM:], u["q"].astype(F32) * u["eg"]], axis=0)
                u["rs"] = _dot(wq.astype(BF16), state[u["h"]].astype(BF16))
            for u in us:
                u["vb"] = (u["uw"][:, :HEAD_DIM] - u["rs"][:CHUNK]).astype(BF16)
                u["o"] = u["rs"][CHUNK:] + _dot(u["attn"], u["vb"])
            for u in us:
                k_dec = (u["kf"] * jnp.exp(u["glast"] - u["gc"])).astype(BF16)
                state[u["h"]] = state[u["h"]] * jnp.exp(u["glast"]) + lax.dot_general(
                    k_dec, u["vb"], tt, preferred_element_type=F32)
            for u in us:
                z = z_ref[u["rows"], hcols(u["h"])].astype(F32)
                out_ref[u["rows"], hcols(u["h"])] = (
                    _rms_scale(u["o"]) * nw * _silu(z)).astype(BF16)
        for h in heads:
            state_sc[h] = state[h]
        return carry

    lax.fori_loop(0, tc // (CHUNK * GDN_CHUNKS_PER_TRIP), trip_body, 0)


def _gdn_call(qkvz, bg, gt, nw, tc):
    bsz, seq, _ = qkvz.shape

    def col(j):
        return pl.BlockSpec((None, tc, GDN_WIDTH), lambda b, i: (b, i, j))

    return pl.pallas_call(
        functools.partial(_gdn_kernel, tc=tc),
        grid=(bsz, seq // tc),
        in_specs=[
            col(0), col(1), col(2), col(3),
            pl.BlockSpec((None, tc, LANES), lambda b, i: (b, i, 0)),
            pl.BlockSpec((None, tc // CHUNK, 2 * GDN_HEADS, CHUNK), lambda b, i: (b, i, 0, 0)),
            _resident((1, HEAD_DIM)),
        ],
        out_specs=pl.BlockSpec((None, tc, GDN_WIDTH), lambda b, i: (b, i, 0)),
        out_shape=jax.ShapeDtypeStruct((bsz, seq, GDN_WIDTH), BF16),
        scratch_shapes=[pltpu.VMEM((GDN_HEADS, HEAD_DIM, HEAD_DIM), F32)],
        compiler_params=_params("arbitrary", "arbitrary"),
        name="gated_delta_rule",
    )(qkvz, qkvz, qkvz, qkvz, bg, gt, nw)


def _pack_bf16_pair(hi, lo):
    hb = lax.bitcast_convert_type(hi.astype(BF16).astype(F32), U32)
    lb = lax.bitcast_convert_type(lo.astype(BF16).astype(F32), U32)
    return (hb & jnp.uint32(0xFFFF0000)) | lax.shift_right_logical(lb, jnp.uint32(16))


def _unpack_bf16_pair(p):
    hi = lax.bitcast_convert_type(p & jnp.uint32(0xFFFF0000), F32).astype(BF16)
    lo = lax.bitcast_convert_type(lax.shift_left(p, jnp.uint32(16)), F32).astype(BF16)
    return hi, lo


def _outproj_kernel(yg_ref, ys_ref, x_ref, wo_ref, n2w_ref, wr_ref, rb_ref,
                    h1_ref, hp_ref, lt_ref):
    h1 = (x_ref[...] + _dot(yg_ref[...], wo_ref[0:GDN_WIDTH, :])
          + _dot(ys_ref[...], wo_ref[GDN_WIDTH:GDN_WIDTH + SC_WIDTH, :]))
    h1_ref[...] = h1
    hn = _rms_scale(h1) * n2w_ref[...]
    hn_hi = hn.astype(BF16)
    hn_lo = (hn - hn_hi.astype(F32)).astype(BF16)
    lg = _dot(hn_hi, wr_ref[...])
    lg = lg[:, :LANES] + lg[:, LANES:] + _dot(hn_lo, wr_ref[:, :LANES])
    lt_ref[...] = lg.T + rb_ref[...]
    half = D_MODEL // 2
    hp_ref[...] = _pack_bf16_pair(hn[:, :half], hn[:, half:])


def _outproj_call(yg, ys, x2d, wo, n2w, wr_t, rb, tm):
    n_tok, d = x2d.shape
    return pl.pallas_call(
        _outproj_kernel,
        grid=(n_tok // tm,),
        in_specs=[
            pl.BlockSpec((tm, GDN_WIDTH), lambda i: (i, 0)),
            pl.BlockSpec((tm, SC_WIDTH), lambda i: (i, 0)),
            pl.BlockSpec((tm, d), lambda i: (i, 0)),
            _resident((GDN_WIDTH + SC_WIDTH, d)),
            _resident((1, d)),
            _resident((d, 2 * LANES)),
            _resident((LANES, 1)),
        ],
        out_specs=[
            pl.BlockSpec((tm, d), lambda i: (i, 0)),
            pl.BlockSpec((tm, d // 2), lambda i: (i, 0)),
            pl.BlockSpec((LANES, tm), lambda i: (0, i)),
        ],
        out_shape=[
            jax.ShapeDtypeStruct((n_tok, d), F32),
            jax.ShapeDtypeStruct((n_tok, d // 2), U32),
            jax.ShapeDtypeStruct((LANES, n_tok), F32),
        ],
        compiler_params=_params("arbitrary"),
        name="out_proj_router",
    )(yg, ys, x2d, wo, n2w, wr_t, rb)


def _first_argmax(vals, iota, size):
    m = jnp.max(vals, axis=0, keepdims=True)
    idx = jnp.min(jnp.where(vals == m, iota, size), axis=0, keepdims=True)
    return m, idx


def _route_kernel(lt_ref, idx_ref, gate_ref, cnt_ref, carry_sc, *, tr):
    @pl.when(pl.program_id(0) == 0)
    def _():
        carry_sc[...] = jnp.zeros_like(carry_sc)

    iota8 = lax.broadcasted_iota(I32, (EXPERTS_PER_GROUP, tr), 0)
    gl = lt_ref[0:N_GROUPS, :]
    gmax, gidx = _first_argmax(gl, iota8, N_GROUPS)
    g_w = 1.0 / jnp.sum(jnp.exp(gl - gmax), axis=0, keepdims=True)

    ig = jnp.zeros((EXPERTS_PER_GROUP, tr), F32)
    for g in range(N_GROUPS):
        lo = N_GROUPS + g * EXPERTS_PER_GROUP
        ig = jnp.where(gidx == g, lt_ref[lo:lo + EXPERTS_PER_GROUP, :], ig)
    pe = jnp.exp(ig - jnp.max(ig, axis=0, keepdims=True))
    p = pe / jnp.sum(pe, axis=0, keepdims=True)
    p1, i1 = _first_argmax(p, iota8, EXPERTS_PER_GROUP)
    p2, i2 = _first_argmax(jnp.where(iota8 == i1, -1.0, p), iota8, EXPERTS_PER_GROUP)
    denom = p1 + p2
    e1 = gidx * EXPERTS_PER_GROUP + i1
    e2 = gidx * EXPERTS_PER_GROUP + i2

    iota_e = lax.broadcasted_iota(I32, (N_EXPERTS, tr), 0)
    oh1 = (iota_e == e1).astype(F32)
    oh2 = (iota_e == e2).astype(F32)
    ohs = oh1 + oh2
    r = lax.broadcasted_iota(I32, (tr, tr), 0)
    c = lax.broadcasted_iota(I32, (tr, tr), 1)
    before = jnp.where(r < c, 1.0, 0.0).astype(BF16)
    base = carry_sc[:, 0:1] + _dot(ohs.astype(BF16), before)
    rank1 = jnp.sum(oh1 * base, axis=0, keepdims=True)
    rank2 = jnp.sum(oh2 * base, axis=0, keepdims=True)
    carry_sc[...] = carry_sc[...] + jnp.sum(ohs, axis=1, keepdims=True)
    cnt_ref[...] = carry_sc[...]

    idx_ref[...] = jnp.zeros_like(idx_ref)
    idx_ref[0:1, :] = e1
    idx_ref[1:2, :] = e2
    idx_ref[2:3, :] = rank1.astype(I32)
    idx_ref[3:4, :] = rank2.astype(I32)
    sub = lax.broadcasted_iota(I32, (LANES, tr), 0)
    gmat = jnp.where(sub == 0, g_w * p1 / denom, jnp.where(sub == 1, g_w * p2 / denom, 0.0))
    gate_ref[...] = gmat.T


def _route_call(lt, tr):
    n_tok = lt.shape[1]
    return pl.pallas_call(
        functools.partial(_route_kernel, tr=tr),
        grid=(n_tok // tr,),
        in_specs=[pl.BlockSpec((LANES, tr), lambda i: (0, i))],
        out_specs=[
            pl.BlockSpec((SUBLANES, tr), lambda i: (0, i)),
            pl.BlockSpec((tr, LANES), lambda i: (i, 0)),
            pl.BlockSpec((N_EXPERTS, LANES), lambda i: (0, 0)),
        ],
        out_shape=[
            jax.ShapeDtypeStruct((SUBLANES, n_tok), I32),
            jax.ShapeDtypeStruct((n_tok, LANES), F32),
            jax.ShapeDtypeStruct((N_EXPERTS, LANES), F32),
        ],
        scratch_shapes=[pltpu.VMEM((N_EXPERTS, LANES), F32)],
        compiler_params=_params("arbitrary"),
        name="moe_route",
    )(lt)


DISPATCH_SLOTS = 3


def _dispatch_kernel(zf_ref, d1_ref, d2_ref, d1p_ref, d2p_ref, hp_ref, xs_ref,
                     stage, zbuf, sem_in, sem_out, sem_z, *, tb, bm):
    i = pl.program_id(0)
    last = pl.num_programs(0) - 1

    def block_load(step):
        s = step % DISPATCH_SLOTS
        return pltpu.make_async_copy(hp_ref.at[pl.ds(step * tb, tb)], stage.at[s], sem_in.at[s])

    def row_copies(step, da_ref, db_ref):
        s = step % DISPATCH_SLOTS
        return [pltpu.make_async_copy(stage.at[s, pl.ds(u, 1)],
                                      xs_ref.at[pl.ds(d_ref[0, 0, u], 1)], sem_out.at[s])
                for u in range(tb) for d_ref in (da_ref, db_ref)]

    def zero_fill(b):
        return pltpu.make_async_copy(zbuf, xs_ref.at[pl.ds(pl.multiple_of(b * bm, bm), bm)], sem_z)

    @pl.when(i == 0)
    def _():
        block_load(i).start()
        zbuf[...] = jnp.zeros_like(zbuf)

        def start(b, carry):
            @pl.when(zf_ref[b] != 0)
            def _():
                zero_fill(b).start()
            return carry

        def wait(b, carry):
            @pl.when(zf_ref[b] != 0)
            def _():
                zero_fill(b).wait()
            return carry

        lax.fori_loop(0, zf_ref.shape[0], start, 0)
        lax.fori_loop(0, zf_ref.shape[0], wait, 0)

    @pl.when(i < last)
    def _():
        block_load(i + 1).start()

    block_load(i).wait()
    for n, cp in enumerate(row_copies(i, d1_ref, d2_ref)):
        cp.start(priority=n % 2)

    @pl.when(i > 0)
    def _():
        for cp in row_copies(i - 1, d1p_ref, d2p_ref):
            cp.wait()

    @pl.when(i == last)
    def _():
        for cp in row_copies(i, d1_ref, d2_ref):
            cp.wait()


def _dispatch_call(zero_flag, d1, d2, hp, n_rows, tb, bm):
    n_tok, width = hp.shape
    cur = pl.BlockSpec((1, 1, tb), lambda i, zf: (i, 0, 0), memory_space=pltpu.SMEM)
    prev = pl.BlockSpec((1, 1, tb), lambda i, zf: (jnp.maximum(i - 1, 0), 0, 0),
                        memory_space=pltpu.SMEM)
    d1 = d1.reshape(n_tok // tb, 1, tb)
    d2 = d2.reshape(n_tok // tb, 1, tb)
    grid_spec = pltpu.PrefetchScalarGridSpec(
        num_scalar_prefetch=1,
        grid=(n_tok // tb,),
        in_specs=[cur, cur, prev, prev, pl.BlockSpec(memory_space=pl.ANY)],
        out_specs=pl.BlockSpec(memory_space=pl.ANY),
        scratch_shapes=[
            pltpu.VMEM((DISPATCH_SLOTS, tb, width), U32),
            pltpu.VMEM((bm, width), U32),
            pltpu.SemaphoreType.DMA((DISPATCH_SLOTS,)),
            pltpu.SemaphoreType.DMA((DISPATCH_SLOTS,)),
            pltpu.SemaphoreType.DMA(()),
        ],
    )
    return pl.pallas_call(
        functools.partial(_dispatch_kernel, tb=tb, bm=bm),
        grid_spec=grid_spec,
        out_shape=jax.ShapeDtypeStruct((n_rows, width), U32),
        compiler_params=_params("arbitrary"),
        name="moe_dispatch",
    )(zero_flag, d1, d2, d1, d2, hp)


def _expert_kernel(be_ref, nu_ref, first_ref, nxt_ref, slot_ref, xs_ref, wg_hbm, wu_hbm, wd_hbm,
                   ys_ref, wg32, wu32, wd32, wgb, wub, wdb, sem):
    i = pl.program_id(0)
    used = i < nu_ref[0]

    def weight_copies(e, s):
        return [pltpu.make_async_copy(src.at[e], dst.at[s], sem.at[s])
                for src, dst in ((wg_hbm, wg32), (wu_hbm, wu32), (wd_hbm, wd32))]

    @pl.when(used & (first_ref[i] != 0))
    def _():
        e = be_ref[i]
        s = slot_ref[i]

        @pl.when(i == 0)
        def _():
            for cp in weight_copies(e, s):
                cp.start()

        for cp in weight_copies(e, s):
            cp.wait()

        @pl.when(nxt_ref[i] >= 0)
        def _():
            for cp in weight_copies(nxt_ref[i], 1 - s):
                cp.start()

        wgb[...] = wg32[s].astype(BF16)
        wub[...] = wu32[s].astype(BF16)
        wdb[...] = wd32[s].astype(BF16)

    @pl.when(used)
    def _():
        hi, lo = _unpack_bf16_pair(xs_ref[...])
        xb = jnp.concatenate([hi, lo], axis=1)
        hid = _silu(_dot(xb, wgb[...])) * _dot(xb, wub[...])
        y = _dot(hid.astype(BF16), wdb[...])
        ys_ref[...] = _pack_bf16_pair(y[:, :D_MODEL // 2], y[:, D_MODEL // 2:])

    @pl.when(jnp.logical_not(used))
    def _():
        ys_ref[...] = jnp.zeros_like(ys_ref)


def _expert_call(block_expert, n_used, first, nxt, slot, xs, wg, wu, wd, bm):
    n_rows, width = xs.shape
    d = 2 * width
    nb = n_rows // bm
    grid_spec = pltpu.PrefetchScalarGridSpec(
        num_scalar_prefetch=5,
        grid=(nb,),
        in_specs=[
            pl.BlockSpec((bm, width), lambda i, be, nu, *_: (jnp.minimum(i, nu[0] - 1), 0)),
            pl.BlockSpec(memory_space=pl.ANY),
            pl.BlockSpec(memory_space=pl.ANY),
            pl.BlockSpec(memory_space=pl.ANY),
        ],
        out_specs=pl.BlockSpec((bm, width), lambda i, *_: (i, 0)),
        scratch_shapes=[
            pltpu.VMEM((2, d, D_EXPERT), F32),
            pltpu.VMEM((2, d, D_EXPERT), F32),
            pltpu.VMEM((2, D_EXPERT, d), F32),
            pltpu.VMEM((d, D_EXPERT), BF16),
            pltpu.VMEM((d, D_EXPERT), BF16),
            pltpu.VMEM((D_EXPERT, d), BF16),
            pltpu.SemaphoreType.DMA((2,)),
        ],
    )
    return pl.pallas_call(
        _expert_kernel,
        grid_spec=grid_spec,
        out_shape=jax.ShapeDtypeStruct((n_rows, width), U32),
        compiler_params=_params("arbitrary"),
        name="moe_experts",
    )(block_expert, n_used, first, nxt, slot, xs, wg, wu, wd)


def _combine_kernel(d1_ref, d2_ref, d1n_ref, d2n_ref, gate_ref, h1_ref, wf_ref, ys_ref, out_ref,
                    buf, sem, *, tc):
    i = pl.program_id(0)
    slot = i % 2

    def row_copies(da_ref, db_ref, s):
        return [pltpu.make_async_copy(ys_ref.at[pl.ds(d_ref[0, 0, u], 1)],
                                      buf.at[s, k, pl.ds(u, 1)], sem.at[s])
                for u in range(tc) for k, d_ref in enumerate((da_ref, db_ref))]

    @pl.when(i == 0)
    def _():
        for n, cp in enumerate(row_copies(d1_ref, d2_ref, 0)):
            cp.start(priority=n % 2)

    @pl.when(i + 1 < pl.num_programs(0))
    def _():
        for n, cp in enumerate(row_copies(d1n_ref, d2n_ref, 1 - slot)):
            cp.start(priority=n % 2)

    for cp in row_copies(d1_ref, d2_ref, slot):
        cp.wait()
    gates = gate_ref[...]
    h = h1_ref[...]
    for k in range(TOP_K):
        hi, lo = _unpack_bf16_pair(buf[slot, k])
        y = jnp.concatenate([hi, lo], axis=1).astype(F32)
        h = h + y * gates[:, k:k + 1]
    out_ref[...] = _rms_scale(h) * wf_ref[...]


def _combine_call(d1, d2, gates, h1, wf, ys, tc):
    n_tok, d = h1.shape
    steps = n_tok // tc
    cur = pl.BlockSpec((1, 1, tc), lambda i: (i, 0, 0), memory_space=pltpu.SMEM)
    nxt = pl.BlockSpec((1, 1, tc), lambda i: (jnp.minimum(i + 1, steps - 1), 0, 0),
                       memory_space=pltpu.SMEM)
    d1 = d1.reshape(steps, 1, tc)
    d2 = d2.reshape(steps, 1, tc)
    return pl.pallas_call(
        functools.partial(_combine_kernel, tc=tc),
        grid=(steps,),
        in_specs=[
            cur, cur, nxt, nxt,
            pl.BlockSpec((tc, LANES), lambda i: (i, 0)),
            pl.BlockSpec((tc, d), lambda i: (i, 0)),
            _resident((1, d)),
            pl.BlockSpec(memory_space=pl.ANY),
        ],
        out_specs=pl.BlockSpec((tc, d), lambda i: (i, 0)),
        out_shape=jax.ShapeDtypeStruct((n_tok, d), F32),
        scratch_shapes=[
            pltpu.VMEM((2, TOP_K, tc, d // 2), U32),
            pltpu.SemaphoreType.DMA((2,)),
        ],
        compiler_params=_params("arbitrary"),
        name="moe_combine",
    )(d1, d2, d1, d2, gates, h1, wf, ys)


def _tiles(seq, n_tok):
    return dict(
        proj=min(256, seq),
        gdn=min(512, seq),
        outp=min(512, n_tok),
        route=min(512, n_tok),
        disp=min(256, n_tok),
        bm=256,
        comb=min(256, n_tok),
    )


def _table_lookup(table, keys):
    e_ids = jnp.arange(table.shape[0], dtype=I32)
    return jnp.sum(jnp.where(e_ids[:, None] == keys[None, :], table[:, None], 0), axis=0)


def _layer(h, n1w, w_in, conv_qkv_w, a_log, dt_bias, gdn_norm_w, conv_sc_w, w_out,
           n2w, wg_r, bg_r, we_r, be_r, w_gate, w_up, w_down, final_w, last):
    bsz, seq, d = h.shape
    n_tok = bsz * seq
    ts = _tiles(seq, n_tok)

    qkvz_cols = 4 * GDN_WIDTH
    sc0 = qkvz_cols + 2 * GDN_HEADS
    w_main = jnp.concatenate([w_in[:, :qkvz_cols], w_in[:, sc0:]], axis=1).astype(BF16)
    w_ba = jnp.pad(w_in[:, qkvz_cols:sc0], ((0, 0), (0, LANES - 2 * GDN_HEADS))).astype(BF16)
    pad8 = (GDN_HEADS, LANES - 2 * GDN_HEADS)
    alog_v = jnp.pad(a_log, pad8).reshape(1, LANES)
    dtb_v = jnp.pad(dt_bias, pad8).reshape(1, LANES)

    qkvz, y_sc, bg = _inproj_call(h, n1w.reshape(1, d), w_main, w_ba, conv_qkv_w, conv_sc_w,
                                  alog_v, dtb_v, ts["proj"])

    gt = jnp.swapaxes(bg[:, :, :2 * GDN_HEADS].reshape(bsz, seq // CHUNK, CHUNK, 2 * GDN_HEADS),
                      2, 3)
    y_gdn = _gdn_call(qkvz, bg, gt, gdn_norm_w.reshape(1, HEAD_DIM), ts["gdn"])

    n_router = N_GROUPS + N_EXPERTS
    wr = jnp.pad(jnp.concatenate([wg_r, we_r], axis=1), ((0, 0), (0, LANES - n_router)))
    wr_hi = wr.astype(BF16)
    wr_t = jnp.concatenate([wr_hi, (wr - wr_hi.astype(F32)).astype(BF16)], axis=1)
    rb = jnp.pad(jnp.concatenate([bg_r, be_r]), (0, LANES - n_router)).reshape(LANES, 1)
    x2d = h.reshape(n_tok, d)
    h1, hp, lt = _outproj_call(y_gdn.reshape(n_tok, GDN_WIDTH), y_sc.reshape(n_tok, SC_WIDTH),
                               x2d, w_out.astype(BF16), n2w.reshape(1, d), wr_t, rb, ts["outp"])

    idx, gates, cnt = _route_call(lt, ts["route"])
    bm = ts["bm"]
    counts = cnt[:, 0].astype(I32)
    padded = (counts + bm - 1) // bm * bm
    pad_end = jnp.cumsum(padded)
    pad_start = pad_end - padded
    d1 = _table_lookup(pad_start, idx[0]) + idx[2]
    d2 = _table_lookup(pad_start, idx[1]) + idx[3]
    nb = n_tok * TOP_K // bm + N_EXPERTS
    block_row = jnp.arange(nb, dtype=I32) * bm
    block_expert = jnp.minimum(
        jnp.sum((pad_end[None, :] <= block_row[:, None]).astype(I32), axis=1), N_EXPERTS - 1)
    n_used = (pad_end[-1:] // bm).astype(I32)

    has_rows = padded > 0
    e_ids = jnp.arange(N_EXPERTS, dtype=I32)
    later = jnp.where(has_rows, e_ids, N_EXPERTS)
    next_used = jnp.concatenate([lax.cummin(later, reverse=True)[1:],
                                 jnp.full((1,), N_EXPERTS, I32)])
    next_used = jnp.where(next_used < N_EXPERTS, next_used, -1)
    rank_used = jnp.cumsum(has_rows.astype(I32)) - 1
    first = (block_row == _table_lookup(pad_start, block_expert)).astype(I32)
    nxt = _table_lookup(next_used, block_expert)
    slot = _table_lookup(rank_used, block_expert) % 2
    is_last = jnp.any((pad_end[None, :] == (block_row + bm)[:, None]) & has_rows[None, :], axis=1)
    zero_flag = (is_last | (block_row >= pad_end[-1])).astype(I32)

    xs = _dispatch_call(zero_flag, d1, d2, hp, nb * bm, ts["disp"], bm)
    ys = _expert_call(block_expert, n_used, first, nxt, slot, xs, w_gate, w_up, w_down, bm)
    out = _combine_call(d1, d2, gates, h1, final_w.reshape(1, d), ys, ts["comb"])
    del last
    return out.reshape(bsz, seq, d)


def kernel(x, norm1_w, w_in, conv_qkv_w, a_log, dt_bias, gdn_norm_w, conv_sc_w, w_out, norm2_w,
           router_group_w, router_group_b, router_expert_w, router_expert_b, w_gate, w_up,
           w_down, final_norm_w):
    depth = norm1_w.shape[0]
    assert depth == 1, "the combine call applies the final RMSNorm, so a single layer is assumed"
    return _layer(x, norm1_w[0], w_in[0], conv_qkv_w[0], a_log[0], dt_bias[0], gdn_norm_w[0],
                  conv_sc_w[0], w_out[0], norm2_w[0], router_group_w[0], router_group_b[0],
                  router_expert_w[0], router_expert_b[0], w_gate[0], w_up[0], w_down[0],
                  final_norm_w, True)
```
